```python
import jax, jax.numpy as jnp
from jax import lax
import numpy as np

D_MODEL = 1024
BATCH = 4
SEQ = 4096
DEPTH = 1

EXPAND = 2
D_MIX = EXPAND * D_MODEL
HEAD_DIM = 64
D_SSD = D_MIX // 2
D_SB = D_MIX - D_SSD
N_SSD_HEADS = D_SSD // HEAD_DIM
N_SB_HEADS = D_SB // HEAD_DIM
SSD_GROUPS = 2
SSD_STATE = 128
CONV_WIDTH = 4
D_CONV = D_SSD + 2 * SSD_GROUPS * SSD_STATE
SSD_CHUNK = 128
SB_BLOCK = 128
D_FF = -(-8 * D_MODEL // (3 * 256)) * 256
N_MOD = 6
EPS = 1e-6
IN_SPLITS = [int(s) for s in np.cumsum([D_SSD, D_CONV, N_SSD_HEADS, D_SB, D_SB])]
D_IN_PROJ = IN_SPLITS[-1] + D_SB

kernel_name = "hymba_ssd_stickbreaking_adaln_block"


def rms_norm(x, w):
    xf = x.astype(jnp.float32)
    y = xf * lax.rsqrt(jnp.mean(xf * xf, axis=-1, keepdims=True) + EPS)
    return (y * w.astype(jnp.float32)).astype(x.dtype)


def causal_depthwise_conv(u, w, b):
    out = lax.conv_general_dilated(
        u, w[:, None, :], window_strides=(1,), padding=[(CONV_WIDTH - 1, 0)],
        dimension_numbers=("NWC", "WIO", "NWC"), feature_group_count=u.shape[-1])
    return out + b


def ssd_chunked(x, dt, a, b_mat, c_mat):
    bsz, seq, n_heads, p = x.shape
    g, n = b_mat.shape[2], b_mat.shape[3]
    e = n_heads // g
    nc, t = seq // SSD_CHUNK, SSD_CHUNK
    xg = (x * dt[..., None]).reshape(bsz, nc, t, g, e, p)
    a_dt = jnp.moveaxis((a * dt).reshape(bsz, nc, t, g, e), 2, -1)
    a_cs = jnp.cumsum(a_dt, axis=-1)
    bm = b_mat.reshape(bsz, nc, t, g, n)
    cm = c_mat.reshape(bsz, nc, t, g, n)
    idx = jnp.arange(t)
    causal = idx[:, None] >= idx[None, :]
    seg = a_cs[..., :, None] - a_cs[..., None, :]
    decay = jnp.exp(jnp.where(causal, seg, -jnp.inf))
    scores = jnp.einsum("bclgn,bcsgn->bcgls", cm, bm)
    y_diag = jnp.einsum("bcgls,bcgels,bcsgep->bclgep", scores, decay, xg)
    decay_states = jnp.exp(a_cs[..., -1:] - a_cs)
    states = jnp.einsum("bclgn,bcgel,bclgep->bcgepn", bm, decay_states, xg)
    chunk_decay = jnp.exp(a_cs[..., -1])

    def step(carry, inp):
        st, dec = inp
        return carry * dec[..., None, None] + st, carry

    init = jnp.zeros_like(states[:, 0])
    _, prev_states = lax.scan(step, init, (jnp.moveaxis(states, 1, 0), jnp.moveaxis(chunk_decay, 1, 0)))
    prev_states = jnp.moveaxis(prev_states, 0, 1)
    y_off = jnp.einsum("bclgn,bcgepn,bcgel->bclgep", cm, prev_states, jnp.exp(a_cs))
    return (y_diag + y_off).reshape(bsz, seq, n_heads, p)


def ssd_mixer(z, xbc, dt_raw, conv_w, conv_b, dt_bias, a_log, d_skip, norm_w):
    bsz, seq, _ = z.shape
    f32 = jnp.float32
    xbc = jax.nn.silu(causal_depthwise_conv(xbc, conv_w, conv_b)).astype(f32)
    xs = xbc[..., :D_SSD]
    bm = xbc[..., D_SSD:D_SSD + SSD_GROUPS * SSD_STATE].reshape(bsz, seq, SSD_GROUPS, SSD_STATE)
    cm = xbc[..., D_SSD + SSD_GROUPS * SSD_STATE:].reshape(bsz, seq, SSD_GROUPS, SSD_STATE)
    xh = xs.reshape(bsz, seq, N_SSD_HEADS, HEAD_DIM)
    dt = jax.nn.softplus(dt_raw.astype(f32) + dt_bias.astype(f32))
    a = -jnp.exp(a_log.astype(f32))
    y = ssd_chunked(xh, dt, a, bm, cm) + d_skip.astype(f32)[:, None] * xh
    y = y.reshape(bsz, seq, D_SSD) * jax.nn.silu(z.astype(f32))
    yg = y.reshape(bsz, seq, SSD_GROUPS, D_SSD // SSD_GROUPS)
    yg = yg * lax.rsqrt(jnp.mean(yg * yg, axis=-1, keepdims=True) + EPS)
    return (yg.reshape(bsz, seq, D_SSD) * norm_w.astype(f32)).astype(z.dtype)


def stick_breaking_mixer(q, k, v, q_norm_w, k_norm_w):
    bsz, seq, _ = q.shape
    f32 = jnp.float32
    q = rms_norm(q.reshape(bsz, seq, N_SB_HEADS, HEAD_DIM), q_norm_w).astype(f32).transpose(0, 2, 1, 3)
    k = rms_norm(k.reshape(bsz, seq, N_SB_HEADS, HEAD_DIM), k_norm_w).astype(f32).transpose(0, 2, 1, 3)
    vh = v.reshape(bsz, seq, N_SB_HEADS, HEAD_DIM).astype(f32).transpose(0, 2, 1, 3)
    scale = HEAD_DIM ** -0.5
    outs = []
    for blk in range(seq // SB_BLOCK):
        start, end = blk * SB_BLOCK, (blk + 1) * SB_BLOCK
        qb, kp, vp = q[:, :, start:end], k[:, :, :end], vh[:, :, :end]
        logits = jnp.einsum("bhqd,bhkd->bhqk", qb, kp) * scale
        t_pos = start + jnp.arange(SB_BLOCK)
        s_pos = jnp.arange(end)
        strict = s_pos[None, :] < t_pos[:, None]
        log_rem = jnp.where(strict, jax.nn.log_sigmoid(-logits), 0.0)
        log_between = lax.cumsum(log_rem, axis=3, reverse=True) - log_rem
        weights = jnp.where(strict, jnp.exp(jax.nn.log_sigmoid(logits) + log_between), 0.0)
        outs.append(jnp.einsum("bhqk,bhkd->bhqd", weights, vp))
    o = jnp.concatenate(outs, axis=2)
    return o.transpose(0, 2, 1, 3).reshape(bsz, seq, D_SB).astype(v.dtype)


def setup_inputs(seed: int = 0) -> dict:
    key = jax.random.key(seed)
    ks = jax.random.split(key, 24)
    f32 = jnp.float32
    nrm = lambda k, shape, s: jax.random.normal(k, shape, f32) * s
    dt0 = jnp.exp(jax.random.uniform(ks[8], (DEPTH, N_SSD_HEADS), f32,
                                     float(np.log(1e-3)), float(np.log(1e-1))))
    dt_bias = dt0 + jnp.log(-jnp.expm1(-dt0))
    return {
        "x": nrm(ks[0], (BATCH, SEQ, D_MODEL), 1.0),
        "c": nrm(ks[1], (BATCH, D_MODEL), 1.0),
        "w_ada": nrm(ks[2], (DEPTH, D_MODEL, N_MOD * D_MODEL), 0.5 * D_MODEL ** -0.5),
        "b_ada": nrm(ks[3], (DEPTH, N_MOD * D_MODEL), 0.01),
        "norm1_w": 1.0 + nrm(ks[4], (DEPTH, D_MODEL), 0.02),
        "w_in": nrm(ks[5], (DEPTH, D_MODEL, D_IN_PROJ), D_MODEL ** -0.5),
        "conv_w": nrm(ks[6], (DEPTH, CONV_WIDTH, D_CONV), CONV_WIDTH ** -0.5),
        "conv_b": nrm(ks[7], (DEPTH, D_CONV), 0.01),
        "dt_bias": dt_bias,
        "a_log": jnp.log(jax.random.uniform(ks[9], (DEPTH, N_SSD_HEADS), f32, 1.0, 16.0)),
        "d_skip": 1.0 + nrm(ks[10], (DEPTH, N_SSD_HEADS), 0.02),
        "ssd_norm_w": 1.0 + nrm(ks[11], (DEPTH, D_SSD), 0.02),
        "q_norm_w": 1.0 + nrm(ks[12], (DEPTH, HEAD_DIM), 0.02),
        "k_norm_w": 1.0 + nrm(ks[13], (DEPTH, HEAD_DIM), 0.02),
        "w_out": nrm(ks[14], (DEPTH, D_MIX, D_MODEL), D_MIX ** -0.5),
        "norm2_w": 1.0 + nrm(ks[15], (DEPTH, D_MODEL), 0.02),
        "w_gate": nrm(ks[16], (DEPTH, D_MODEL, D_FF), D_MODEL ** -0.5),
        "w_up": nrm(ks[17], (DEPTH, D_MODEL, D_FF), D_MODEL ** -0.5),
        "w_down": nrm(ks[18], (DEPTH, D_FF, D_MODEL), D_FF ** -0.5),
    }


def reference(x, c, w_ada, b_ada, norm1_w, w_in, conv_w, conv_b, dt_bias, a_log, d_skip,
              ssd_norm_w, q_norm_w, k_norm_w, w_out, norm2_w, w_gate, w_up, w_down):
    cond = jax.nn.silu(c)
    for layer in range(DEPTH):
        mod = (cond @ w_ada[layer] + b_ada[layer])[:, None, :]
        sh1, sc1, g1, sh2, sc2, g2 = jnp.split(mod, N_MOD, axis=-1)
        h = rms_norm(x, norm1_w[layer]) * (1.0 + sc1) + sh1
        proj = h @ w_in[layer]
        z, xbc, dt_raw, q, k, v = jnp.split(proj, IN_SPLITS, axis=-1)
        y_ssd = ssd_mixer(z, xbc, dt_raw, conv_w[layer], conv_b[layer], dt_bias[layer],
                          a_log[layer], d_skip[layer], ssd_norm_w[layer])
        y_sb = stick_breaking_mixer(q, k, v, q_norm_w[layer], k_norm_w[layer])
        mix = jnp.concatenate([y_ssd, y_sb], axis=-1) @ w_out[layer]
        x = x + g1 * mix
        h = rms_norm(x, norm2_w[layer]) * (1.0 + sc2) + sh2
        ffn = (jax.nn.silu(h @ w_gate[layer]) * (h @ w_up[layer])) @ w_down[layer]
        x = x + g2 * ffn
    return x
```

```python
import functools

import jax
import jax.numpy as jnp
from jax import lax
from jax.experimental import pallas as pl
from jax.experimental.pallas import tpu as pltpu

F32 = jnp.float32
BF16 = jnp.bfloat16

HEAD_DIM = 64
SSD_GROUPS = 2
SSD_STATE = 128
CONV_WIDTH = 4
SSD_CHUNK = 128
SB_BLOCK = 128
N_MOD = 6
EPS = 1e-6
LANES = 128
VMEM_LIMIT_BYTES = 56 * 1024 * 1024
NEG_BIG = -1e30


def _split_bf16(x, parts):
    out = []
    r = x
    for i in range(parts):
        p = r.astype(BF16)
        out.append(p)
        if i + 1 < parts:
            r = r - p.astype(F32)
    return out


def _dot_split_lhs(x, m, parts):
    acc = None
    for p in _split_bf16(x, parts):
        d = jnp.dot(p, m, preferred_element_type=F32)
        acc = d if acc is None else acc + d
    return acc


def _dot_split_rhs(m, x, parts):
    acc = None
    for p in _split_bf16(x, parts):
        d = jnp.dot(m, p, preferred_element_type=F32)
        acc = d if acc is None else acc + d
    return acc


def _silu(x):
    return x / (1.0 + jnp.exp(-x))


def _softplus(x):
    return jnp.maximum(x, 0.0) + jnp.log(1.0 + jnp.exp(-jnp.abs(x)))


def _params(sem):
    return pltpu.CompilerParams(dimension_semantics=sem, vmem_limit_bytes=VMEM_LIMIT_BYTES)


def _adaln_kernel(c_ref, w_ref, b_ref, o_ref):
    cond = _silu(c_ref[...])
    o_ref[...] = _dot_split_lhs_f32(cond, w_ref[...]) + b_ref[...]


def _dot_split_lhs_f32(x, w):
    xs = _split_bf16(x, 3)
    ws = _split_bf16(w, 3)
    acc = None
    for i in range(3):
        for j in range(3 - i):
            d = jnp.dot(xs[i], ws[j], preferred_element_type=F32)
            acc = d if acc is None else acc + d
    return acc


def _adaln(c, w, b):
    bsz, d = c.shape
    n = w.shape[1]
    tn = d
    return pl.pallas_call(
        _adaln_kernel,
        grid=(n // tn,),
        in_specs=[pl.BlockSpec((bsz, d), lambda j: (0, 0)),
                  pl.BlockSpec((d, tn), lambda j: (0, j)),
                  pl.BlockSpec((1, tn), lambda j: (0, j))],
        out_specs=pl.BlockSpec((bsz, tn), lambda j: (0, j)),
        out_shape=jax.ShapeDtypeStruct((bsz, n), F32),
        compiler_params=_params(("arbitrary",)),
        name="adaln",
    )(c, w, b.reshape(1, n))


def _inproj_kernel(splits, x_ref, mod_ref, nw_ref, w_ref, z_ref, xbc_ref, dt_ref, q_ref, k_ref, v_ref, h_scr):
    x = x_ref[...]
    ms = jnp.mean(x * x, axis=-1, keepdims=True)
    y = x * lax.rsqrt(ms + EPS) * nw_ref[...]
    h_scr[...] = (y * (1.0 + mod_ref[0, 1:2, :]) + mod_ref[0, 0:1, :]).astype(BF16)
    outs = (z_ref, xbc_ref, dt_ref, q_ref, k_ref, v_ref)
    lo = 0
    for o_ref, width in zip(outs, splits):
        step = 512 if width % 512 == 0 else width
        for s in range(0, width, step):
            r = jnp.dot(h_scr[...], w_ref[:, lo + s:lo + s + step], preferred_element_type=F32)
            o_ref[:, s:s + step] = r.astype(o_ref.dtype)
        lo += width


def _inproj(x2d, mod3, norm_w, w_cat, splits, seq, tm):
    t, d = x2d.shape
    per_b = seq // tm
    dtypes = (BF16, BF16, F32, BF16, BF16, BF16)
    return pl.pallas_call(
        functools.partial(_inproj_kernel, splits),
        grid=(t // tm,),
        in_specs=[pl.BlockSpec((tm, d), lambda i: (i, 0)),
                  pl.BlockSpec((1, N_MOD, d), lambda i: (i // per_b, 0, 0)),
                  pl.BlockSpec((1, d), lambda i: (0, 0)),
                  pl.BlockSpec(w_cat.shape, lambda i: (0, 0), pipeline_mode=pl.Buffered(1))],
        out_specs=[pl.BlockSpec((tm, w), lambda i: (i, 0)) for w in splits],
        out_shape=[jax.ShapeDtypeStruct((t, w), dt) for w, dt in zip(splits, dtypes)],
        scratch_shapes=[pltpu.VMEM((tm, d), BF16)],
        compiler_params=_params(("arbitrary",)),
        name="inproj",
    )(x2d, mod3, norm_w.reshape(1, d), w_cat)


def _ssd_kernel(d_ssd, z_ref, xbc_ref, dt_ref, cw_ref, cb_ref, dtb_ref, alog_ref, dskip_ref, nw_ref,
                shift_ref, ltri_ref, e64_ref, e128_ref, o_ref, xbc_scr, state_scr):
    t = SSD_CHUNK
    n = SSD_STATE
    gw = d_ssd // SSD_GROUPS
    chunk = pl.program_id(1)
    cur = chunk % 2

    @pl.when(chunk == 0)
    def _():
        xbc_scr[1] = jnp.zeros(xbc_scr.shape[1:], BF16)
        state_scr[...] = jnp.zeros(state_scr.shape, F32)

    xbc_scr[cur] = xbc_ref[...]
    u_cur = xbc_scr[cur]
    u_prev = xbc_scr[1 - cur]
    conv = cb_ref[...] + cw_ref[CONV_WIDTH - 1:CONV_WIDTH, :] * u_cur.astype(F32)
    for k in range(CONV_WIDTH - 1):
        shifted = (jnp.dot(shift_ref[k, :, 0:t], u_prev, preferred_element_type=F32)
                   + jnp.dot(shift_ref[k, :, t:2 * t], u_cur, preferred_element_type=F32))
        conv = conv + cw_ref[k:k + 1, :] * shifted
    act = _silu(conv)
    xs = act[:, :d_ssd]
    bm = act[:, d_ssd:d_ssd + SSD_GROUPS * n]
    cm = act[:, d_ssd + SSD_GROUPS * n:]

    dt = _softplus(dt_ref[...] + dtb_ref[...])
    a = -jnp.exp(alog_ref[...])
    a_cs = _dot_split_rhs(ltri_ref[...], a * dt, 3)
    e64 = e64_ref[...]
    dt_full = _dot_split_lhs(dt, e64, 3)
    acs_full = _dot_split_lhs(a_cs, e64, 3)
    acs_col = _dot_split_lhs(a_cs, e128_ref[...], 3)
    acs_row = a_cs.T
    a_last = acs_full[t - 1:t, :]
    xg = xs * dt_full
    xg_b = xg.astype(BF16)
    xdec_b = (xg * jnp.exp(a_last - acs_full)).astype(BF16)
    exp_acs = jnp.exp(acs_full)
    chunk_decay = jnp.exp(a_last)

    li = lax.broadcasted_iota(jnp.int32, (t, t), 0)
    si = lax.broadcasted_iota(jnp.int32, (t, t), 1)
    causal = li >= si
    lane = lax.broadcasted_iota(jnp.int32, (t, LANES), 1)
    low_half = lane < HEAD_DIM

    heads_per_group = gw // HEAD_DIM
    nt_dims = (((1,), (1,)), ((), ()))
    y_parts = []
    for g in range(SSD_GROUPS):
        bm_g = bm[:, g * n:(g + 1) * n]
        cm_g = cm[:, g * n:(g + 1) * n].astype(BF16)
        scores = lax.dot_general(cm_g, bm_g.astype(BF16), nt_dims, preferred_element_type=F32)
        st = state_scr[:, g * gw:(g + 1) * gw]
        y_off = jnp.dot(cm_g, st.astype(BF16), preferred_element_type=F32) * exp_acs[:, g * gw:(g + 1) * gw]
        new_states = jnp.dot(bm_g.T.astype(BF16), xdec_b[:, g * gw:(g + 1) * gw], preferred_element_type=F32)
        state_scr[:, g * gw:(g + 1) * gw] = st * chunk_decay[:, g * gw:(g + 1) * gw] + new_states
        for pr in range(heads_per_group // 2):
            col = g * gw + pr * LANES
            xg_pair = xg_b[:, col:col + LANES]
            y_pair = None
            for j in range(2):
                h = (col // HEAD_DIM) + j
                seg = acs_col[:, h * LANES:(h + 1) * LANES] - acs_row[h:h + 1, :]
                decay = jnp.exp(jnp.where(causal, seg, NEG_BIG))
                m_h = (scores * decay).astype(BF16)
                x_h = jnp.where(low_half if j == 0 else jnp.logical_not(low_half), xg_pair, jnp.zeros_like(xg_pair))
                d = jnp.dot(m_h, x_h, preferred_element_type=F32)
                y_pair = d if y_pair is None else y_pair + d
            y_parts.append(y_pair + y_off[:, pr * LANES:(pr + 1) * LANES])
    y = jnp.concatenate(y_parts, axis=-1) + dskip_ref[...] * xs

    y = y * _silu(z_ref[...].astype(F32))
    outs = []
    for g in range(SSD_GROUPS):
        yg = y[:, g * gw:(g + 1) * gw]
        ms = jnp.mean(yg * yg, axis=-1, keepdims=True)
        outs.append(yg * lax.rsqrt(ms + EPS))
    o_ref[...] = (jnp.concatenate(outs, axis=-1) * nw_ref[...]).astype(o_ref.dtype)


def _ssd(z, xbc, dt, conv_w, conv_b, dt_bias, a_log, d_skip, norm_w, bsz, seq):
    t_all, d_ssd = z.shape
    d_conv = xbc.shape[1]
    n_heads = d_ssd // HEAD_DIM
    t = SSD_CHUNK
    nc = seq // t

    def pad_row(v):
        return jnp.zeros((1, LANES), F32).at[0, :n_heads].set(v.astype(F32))

    rows = jnp.arange(t)[:, None]
    cols = jnp.arange(2 * t)[None, :]
    shifts = jnp.stack([(cols == rows + t - (CONV_WIDTH - 1) + k) for k in range(CONV_WIDTH - 1)]).astype(BF16)
    ltri = (jnp.arange(t)[None, :] <= rows).astype(BF16)
    hrow = jnp.arange(LANES)[:, None]
    e64 = (jnp.arange(d_ssd)[None, :] // HEAD_DIM == hrow).astype(BF16)
    e128 = (jnp.arange(n_heads * LANES)[None, :] // LANES == hrow).astype(BF16)
    dskip_full = jnp.repeat(d_skip.astype(F32), HEAD_DIM).reshape(1, d_ssd)

    blk = lambda w: pl.BlockSpec((t, w), lambda b, c: (b * nc + c, 0))
    full = lambda arr: pl.BlockSpec(arr.shape, lambda b, c: (0,) * arr.ndim)
    consts = (conv_w.astype(F32), conv_b.reshape(1, d_conv).astype(F32), pad_row(dt_bias), pad_row(a_log),
              dskip_full, norm_w.reshape(1, d_ssd).astype(F32), shifts, ltri, e64, e128)
    return pl.pallas_call(
        functools.partial(_ssd_kernel, d_ssd),
        grid=(bsz, nc),
        in_specs=[blk(d_ssd), blk(d_conv), blk(LANES)] + [full(a) for a in consts],
        out_specs=blk(d_ssd),
        out_shape=jax.ShapeDtypeStruct((t_all, d_ssd), BF16),
        scratch_shapes=[pltpu.VMEM((2, t, d_conv), BF16), pltpu.VMEM((SSD_STATE, d_ssd), F32)],
        compiler_params=_params(("arbitrary", "arbitrary")),
        name="ssd",
    )(z, xbc, dt, *consts)


def _sb_block(q_pair, k_blk, v_pair, uo, carry, strict):
    c_lo, c_hi, acc = carry
    nt_dims = (((1,), (1,)), ((), ()))
    new_c = []
    for q_j, v_j, c_j in ((q_pair[0], v_pair[0], c_lo), (q_pair[1], v_pair[1], c_hi)):
        z = lax.dot_general(q_j, k_blk, nt_dims, preferred_element_type=F32)
        lr = -_softplus(z)
        if strict is not None:
            lr = jnp.where(strict, lr, 0.0)
        rt = _dot_split_lhs(lr, uo, 2)
        w = jnp.exp(z + rt[:, :SB_BLOCK] + c_j)
        if strict is not None:
            w = jnp.where(strict, w, 0.0)
        acc = acc + jnp.dot(w.astype(BF16), v_j, preferred_element_type=F32)
        new_c.append(c_j + rt[:, SB_BLOCK:])
    return new_c[0], new_c[1], acc


def _sbattn_kernel(seq, q_ref, k_ref, v_ref, qw_ref, kw_ref, bd_ref, uo_ref, o_ref,
                   qlo_scr, qhi_scr, kn_scr, vlo_scr, vhi_scr):
    blk = SB_BLOCK
    rows = 512
    lane = lax.broadcasted_iota(jnp.int32, (rows, LANES), 1)
    low_half = lane < HEAD_DIM
    scale = HEAD_DIM ** -0.5

    def prep(i, _):
        r0 = pl.multiple_of(i * rows, rows)
        sl = pl.ds(r0, rows)

        def normed(ref, w_ref):
            xf = ref[sl, :].astype(F32)
            ms = _dot_split_lhs(xf * xf, bd_ref[...], 2) * (1.0 / HEAD_DIM)
            return xf * lax.rsqrt(ms + EPS) * w_ref[...]

        qn = (normed(q_ref, qw_ref) * scale).astype(BF16)
        zero = jnp.zeros_like(qn)
        qlo_scr[sl, :] = jnp.where(low_half, qn, zero)
        qhi_scr[sl, :] = jnp.where(low_half, zero, qn)
        kn_scr[sl, :] = normed(k_ref, kw_ref).astype(BF16)
        vv = v_ref[sl, :]
        vlo_scr[sl, :] = jnp.where(low_half, vv, zero)
        vhi_scr[sl, :] = jnp.where(low_half, zero, vv)
        return 0

    lax.fori_loop(0, seq // rows, prep, 0)

    ti = lax.broadcasted_iota(jnp.int32, (blk, blk), 0)
    si = lax.broadcasted_iota(jnp.int32, (blk, blk), 1)
    strict = si < ti
    uo = uo_ref[...]

    def q_block(qb, _):
        q0 = pl.multiple_of(qb * blk, blk)
        q_pair = (qlo_scr[pl.ds(q0, blk), :], qhi_scr[pl.ds(q0, blk), :])

        def kv(k0):
            return kn_scr[pl.ds(k0, blk), :], (vlo_scr[pl.ds(k0, blk), :], vhi_scr[pl.ds(k0, blk), :])

        zeros = jnp.zeros((blk, blk), F32)
        k_blk, v_pair = kv(q0)
        carry = _sb_block(q_pair, k_blk, v_pair, uo, (zeros, zeros, zeros), strict)

        def k_step(i, carry):
            k0 = pl.multiple_of((qb - 1 - i) * blk, blk)
            k_blk, v_pair = kv(k0)
            return _sb_block(q_pair, k_blk, v_pair, uo, carry, None)

        carry = lax.fori_loop(0, qb, k_step, carry)
        o_ref[pl.ds(q0, blk), :] = carry[2].astype(o_ref.dtype)
        return 0

    lax.fori_loop(0, seq // blk, q_block, 0)


def _sbattn(q, k, v, q_norm_w, k_norm_w, bsz, seq):
    t_all, d_sb = q.shape
    pairs = d_sb // LANES
    blk = SB_BLOCK
    qw = jnp.tile(q_norm_w.astype(F32), LANES // HEAD_DIM).reshape(1, LANES)
    kw = jnp.tile(k_norm_w.astype(F32), LANES // HEAD_DIM).reshape(1, LANES)
    li = jnp.arange(LANES)
    bd = (li[:, None] // HEAD_DIM == li[None, :] // HEAD_DIM).astype(BF16)
    si = jnp.arange(blk)
    uo = jnp.concatenate([(si[:, None] >= si[None, :]), jnp.ones((blk, blk), bool)], axis=1).astype(BF16)
    io = pl.BlockSpec((seq, LANES), lambda b, p: (b, p))
    full = lambda arr: pl.BlockSpec(arr.shape, lambda b, p: (0,) * arr.ndim)
    return pl.pallas_call(
        functools.partial(_sbattn_kernel, seq),
        grid=(bsz, pairs),
        in_specs=[io, io, io, full(qw), full(kw), full(bd), full(uo)],
        out_specs=io,
        out_shape=jax.ShapeDtypeStruct((t_all, d_sb), BF16),
        scratch_shapes=[pltpu.VMEM((seq, LANES), BF16) for _ in range(5)],
        compiler_params=_params(("arbitrary", "arbitrary")),
        name="sbattn",
    )(q, k, v, qw, kw, bd, uo)


def _outffn_kernel(ff_chunk, x_ref, ys_ref, yb_ref, mod_ref, nw_ref, wo_ref, wg_ref, wu_ref, wd_ref, o_ref):
    d_ssd = ys_ref.shape[1]
    mix = jnp.dot(ys_ref[...], wo_ref[0:d_ssd, :], preferred_element_type=F32)
    mix = mix + jnp.dot(yb_ref[...], wo_ref[d_ssd:, :], preferred_element_type=F32)
    x1 = x_ref[...] + mod_ref[0, 2:3, :] * mix
    ms = jnp.mean(x1 * x1, axis=-1, keepdims=True)
    y = x1 * lax.rsqrt(ms + EPS) * nw_ref[...]
    h = (y * (1.0 + mod_ref[0, 4:5, :]) + mod_ref[0, 3:4, :]).astype(BF16)
    d_ff = wg_ref.shape[1]
    ffn = None
    for s in range(0, d_ff, ff_chunk):
        gate = jnp.dot(h, wg_ref[:, s:s + ff_chunk], preferred_element_type=F32)
        up = jnp.dot(h, wu_ref[:, s:s + ff_chunk], preferred_element_type=F32)
        act = (_silu(gate) * up).astype(BF16)
        d = jnp.dot(act, wd_ref[s:s + ff_chunk, :], preferred_element_type=F32)
        ffn = d if ffn is None else ffn + d
    o_ref[...] = x1 + mod_ref[0, 5:6, :] * ffn


def _outffn(x2d, y_ssd, y_sb, mod3, norm_w, w_out, w_gate, w_up, w_down, seq, tm):
    t, d = x2d.shape
    per_b = seq // tm
    d_ff = w_gate.shape[1]
    ff_chunk = d_ff // 2
    row = lambda w: pl.BlockSpec((tm, w), lambda i: (i, 0))
    resident = lambda arr: pl.BlockSpec(arr.shape, lambda i: (0, 0), pipeline_mode=pl.Buffered(1))
    return pl.pallas_call(
        functools.partial(_outffn_kernel, ff_chunk),
        grid=(t // tm,),
        in_specs=[row(d), row(y_ssd.shape[1]), row(y_sb.shape[1]),
                  pl.BlockSpec((1, N_MOD, d), lambda i: (i // per_b, 0, 0)),
                  pl.BlockSpec((1, d), lambda i: (0, 0)),
                  resident(w_out), resident(w_gate), resident(w_up), resident(w_down)],
        out_specs=row(d),
        out_shape=jax.ShapeDtypeStruct((t, d), F32),
        compiler_params=_params(("arbitrary",)),
        name="outffn",
    )(x2d, y_ssd, y_sb, mod3, norm_w.reshape(1, d), w_out, w_gate, w_up, w_down)


def kernel(x, c, w_ada, b_ada, norm1_w, w_in, conv_w, conv_b, dt_bias, a_log, d_skip, ssd_norm_w, q_norm_w,
           k_norm_w, w_out, norm2_w, w_gate, w_up, w_down):
    bsz, seq, d = x.shape
    depth = w_ada.shape[0]
    n_heads = dt_bias.shape[1]
    d_ssd = n_heads * HEAD_DIM
    d_conv = conv_w.shape[2]
    d_sb = (w_in.shape[2] - d_ssd - d_conv - n_heads) // 3
    dt_lo = d_ssd + d_conv
    splits = (d_ssd, d_conv, LANES, d_sb, d_sb, d_sb)
    tm = 512

    x2d = x.reshape(bsz * seq, d)
    cond_in = c.astype(F32)
    for layer in range(depth):
        mod3 = _adaln(cond_in, w_ada[layer], b_ada[layer]).reshape(bsz, N_MOD, d)
        w_l = w_in[layer]
        w_dt = jnp.pad(w_l[:, dt_lo:dt_lo + n_heads], ((0, 0), (0, LANES - n_heads)))
        w_cat = jnp.concatenate([w_l[:, :dt_lo], w_dt, w_l[:, dt_lo + n_heads:]], axis=1).astype(BF16)
        z, xbc, dt, q, k, v = _inproj(x2d, mod3, norm1_w[layer], w_cat, splits, seq, tm)
        y_ssd = _ssd(z, xbc, dt, conv_w[layer], conv_b[layer], dt_bias[layer], a_log[layer], d_skip[layer],
                     ssd_norm_w[layer], bsz, seq)
        y_sb = _sbattn(q, k, v, q_norm_w[layer], k_norm_w[layer], bsz, seq)
        x2d = _outffn(x2d, y_ssd, y_sb, mod3, norm2_w[layer], w_out[layer].astype(BF16),
                      w_gate[layer].astype(BF16), w_up[layer].astype(BF16), w_down[layer].astype(BF16), seq, tm)
    return x2d.reshape(bsz, seq, d)
```

```python
import functools

import jax
import jax.numpy as jnp
from jax import lax
from jax.experimental import pallas as pl
from jax.experimental.pallas import tpu as pltpu

F32 = jnp.float32
BF16 = jnp.bfloat16

HEAD_DIM = 64
SSD_GROUPS = 2
SSD_STATE = 128
CONV_WIDTH = 4
SSD_CHUNK = 128
SB_BLOCK = 128
N_MOD = 6
EPS = 1e-6
LANES = 128
VMEM_LIMIT_BYTES = 56 * 1024 * 1024
NEG_BIG = -1e30
EXP_ZERO_BELOW = 105.0


def _split_bf16(x, parts):
    out = []
    r = x
    for i in range(parts):
        p = r.astype(BF16)
        out.append(p)
        if i + 1 < parts:
            r = r - p.astype(F32)
    return out


def _dot_split_lhs(x, m, parts):
    acc = None
    for p in _split_bf16(x, parts):
        d = jnp.dot(p, m, preferred_element_type=F32)
        acc = d if acc is None else acc + d
    return acc


def _dot_split_rhs(m, x, parts):
    acc = None
    for p in _split_bf16(x, parts):
        d = jnp.dot(m, p, preferred_element_type=F32)
        acc = d if acc is None else acc + d
    return acc


def _silu(x):
    return x / (1.0 + jnp.exp(-x))


def _softplus(x):
    return jnp.maximum(x, 0.0) + jnp.log(1.0 + jnp.exp(-jnp.abs(x)))


def _params(sem):
    return pltpu.CompilerParams(dimension_semantics=sem, vmem_limit_bytes=VMEM_LIMIT_BYTES)


def _adaln_kernel(c_ref, w_ref, b_ref, o_ref):
    cond = _silu(c_ref[...])
    o_ref[...] = _dot_split_lhs_f32(cond, w_ref[...]) + b_ref[...]


def _dot_split_lhs_f32(x, w):
    xs = _split_bf16(x, 3)
    ws = _split_bf16(w, 3)
    acc = None
    for i in range(3):
        for j in range(3 - i):
            d = jnp.dot(xs[i], ws[j], preferred_element_type=F32)
            acc = d if acc is None else acc + d
    return acc


def _adaln(c, w, b):
    bsz, d = c.shape
    n = w.shape[1]
    tn = d
    return pl.pallas_call(
        _adaln_kernel,
        grid=(n // tn,),
        in_specs=[pl.BlockSpec((bsz, d), lambda j: (0, 0)),
                  pl.BlockSpec((d, tn), lambda j: (0, j)),
                  pl.BlockSpec((1, tn), lambda j: (0, j))],
        out_specs=pl.BlockSpec((bsz, tn), lambda j: (0, j)),
        out_shape=jax.ShapeDtypeStruct((bsz, n), F32),
        compiler_params=_params(("arbitrary",)),
        name="adaln",
    )(c, w, b.reshape(1, n))


def _inproj_kernel(splits, x_ref, mod_ref, nw_ref, w_ref, z_ref, xbc_ref, dt_ref, q_ref, k_ref, v_ref, h_scr):
    x = x_ref[...]
    ms = jnp.mean(x * x, axis=-1, keepdims=True)
    y = x * lax.rsqrt(ms + EPS) * nw_ref[...]
    h_scr[...] = (y * (1.0 + mod_ref[0, 1:2, :]) + mod_ref[0, 0:1, :]).astype(BF16)
    outs = (z_ref, xbc_ref, dt_ref, q_ref, k_ref, v_ref)
    lo = 0
    for o_ref, width in zip(outs, splits):
        step = 512 if width % 512 == 0 else width
        for s in range(0, width, step):
            r = jnp.dot(h_scr[...], w_ref[:, lo + s:lo + s + step], preferred_element_type=F32)
            o_ref[:, s:s + step] = r.astype(o_ref.dtype)
        lo += width


def _inproj(x2d, mod3, norm_w, w_cat, splits, seq, tm):
    t, d = x2d.shape
    per_b = seq // tm
    dtypes = (BF16, BF16, F32, BF16, BF16, BF16)
    return pl.pallas_call(
        functools.partial(_inproj_kernel, splits),
        grid=(t // tm,),
        in_specs=[pl.BlockSpec((tm, d), lambda i: (i, 0)),
                  pl.BlockSpec((1, N_MOD, d), lambda i: (i // per_b, 0, 0)),
                  pl.BlockSpec((1, d), lambda i: (0, 0)),
                  pl.BlockSpec(w_cat.shape, lambda i: (0, 0), pipeline_mode=pl.Buffered(1))],
        out_specs=[pl.BlockSpec((tm, w), lambda i: (i, 0)) for w in splits],
        out_shape=[jax.ShapeDtypeStruct((t, w), dt) for w, dt in zip(splits, dtypes)],
        scratch_shapes=[pltpu.VMEM((tm, d), BF16)],
        compiler_params=_params(("arbitrary",)),
        name="inproj",
    )(x2d, mod3, norm_w.reshape(1, d), w_cat)


def _ssd_kernel(d_ssd, z_ref, xbc_ref, dt_ref, cw_ref, cb_ref, dtb_ref, alog_ref, dskip_ref, nw_ref,
                shift_ref, ltri_ref, e64_ref, e128_ref, o_ref, xbc_scr, state_scr):
    t = SSD_CHUNK
    n = SSD_STATE
    gw = d_ssd // SSD_GROUPS
    chunk = pl.program_id(1)
    cur = chunk % 2

    @pl.when(chunk == 0)
    def _():
        xbc_scr[1] = jnp.zeros(xbc_scr.shape[1:], BF16)
        state_scr[...] = jnp.zeros(state_scr.shape, F32)

    xbc_scr[cur] = xbc_ref[...]
    u_cur = xbc_scr[cur]
    u_prev = xbc_scr[1 - cur]
    conv = cb_ref[...] + cw_ref[CONV_WIDTH - 1:CONV_WIDTH, :] * u_cur.astype(F32)
    for k in range(CONV_WIDTH - 1):
        shifted = (jnp.dot(shift_ref[k, :, 0:t], u_prev, preferred_element_type=F32)
                   + jnp.dot(shift_ref[k, :, t:2 * t], u_cur, preferred_element_type=F32))
        conv = conv + cw_ref[k:k + 1, :] * shifted
    act = _silu(conv)
    xs = act[:, :d_ssd]
    bm = act[:, d_ssd:d_ssd + SSD_GROUPS * n]
    cm = act[:, d_ssd + SSD_GROUPS * n:]

    dt = _softplus(dt_ref[...] + dtb_ref[...])
    a = -jnp.exp(alog_ref[...])
    a_cs = _dot_split_rhs(ltri_ref[...], a * dt, 3)
    e64 = e64_ref[...]
    dt_full = _dot_split_lhs(dt, e64, 3)
    acs_full = _dot_split_lhs(a_cs, e64, 3)
    acs_col = _dot_split_lhs(a_cs, e128_ref[...], 3)
    acs_row = a_cs.T
    a_last = acs_full[t - 1:t, :]
    xg = xs * dt_full
    xg_b = xg.astype(BF16)
    xdec_b = (xg * jnp.exp(a_last - acs_full)).astype(BF16)
    exp_acs = jnp.exp(acs_full)
    chunk_decay = jnp.exp(a_last)

    li = lax.broadcasted_iota(jnp.int32, (t, t), 0)
    si = lax.broadcasted_iota(jnp.int32, (t, t), 1)
    causal = li >= si
    lane = lax.broadcasted_iota(jnp.int32, (t, LANES), 1)
    low_half = lane < HEAD_DIM

    heads_per_group = gw // HEAD_DIM
    nt_dims = (((1,), (1,)), ((), ()))
    y_parts = []
    for g in range(SSD_GROUPS):
        bm_g = bm[:, g * n:(g + 1) * n]
        cm_g = cm[:, g * n:(g + 1) * n].astype(BF16)
        scores = lax.dot_general(cm_g, bm_g.astype(BF16), nt_dims, preferred_element_type=F32)
        st = state_scr[:, g * gw:(g + 1) * gw]
        y_off = jnp.dot(cm_g, st.astype(BF16), preferred_element_type=F32) * exp_acs[:, g * gw:(g + 1) * gw]
        new_states = jnp.dot(bm_g.T.astype(BF16), xdec_b[:, g * gw:(g + 1) * gw], preferred_element_type=F32)
        state_scr[:, g * gw:(g + 1) * gw] = st * chunk_decay[:, g * gw:(g + 1) * gw] + new_states
        for pr in range(heads_per_group // 2):
            col = g * gw + pr * LANES
            xg_pair = xg_b[:, col:col + LANES]
            y_pair = None
            for j in range(2):
                h = (col // HEAD_DIM) + j
                seg = acs_col[:, h * LANES:(h + 1) * LANES] - acs_row[h:h + 1, :]
                decay = jnp.exp(jnp.where(causal, seg, NEG_BIG))
                m_h = (scores * decay).astype(BF16)
                x_h = jnp.where(low_half if j == 0 else jnp.logical_not(low_half), xg_pair, jnp.zeros_like(xg_pair))
                d = jnp.dot(m_h, x_h, preferred_element_type=F32)
                y_pair = d if y_pair is None else y_pair + d
            y_parts.append(y_pair + y_off[:, pr * LANES:(pr + 1) * LANES])
    y = jnp.concatenate(y_parts, axis=-1) + dskip_ref[...] * xs

    y = y * _silu(z_ref[...].astype(F32))
    outs = []
    for g in range(SSD_GROUPS):
        yg = y[:, g * gw:(g + 1) * gw]
        ms = jnp.mean(yg * yg, axis=-1, keepdims=True)
        outs.append(yg * lax.rsqrt(ms + EPS))
    o_ref[...] = (jnp.concatenate(outs, axis=-1) * nw_ref[...]).astype(o_ref.dtype)


def _ssd(z, xbc, dt, conv_w, conv_b, dt_bias, a_log, d_skip, norm_w, bsz, seq):
    t_all, d_ssd = z.shape
    d_conv = xbc.shape[1]
    n_heads = d_ssd // HEAD_DIM
    t = SSD_CHUNK
    nc = seq // t

    def pad_row(v):
        return jnp.zeros((1, LANES), F32).at[0, :n_heads].set(v.astype(F32))

    rows = jnp.arange(t)[:, None]
    cols = jnp.arange(2 * t)[None, :]
    shifts = jnp.stack([(cols == rows + t - (CONV_WIDTH - 1) + k) for k in range(CONV_WIDTH - 1)]).astype(BF16)
    ltri = (jnp.arange(t)[None, :] <= rows).astype(BF16)
    hrow = jnp.arange(LANES)[:, None]
    e64 = (jnp.arange(d_ssd)[None, :] // HEAD_DIM == hrow).astype(BF16)
    e128 = (jnp.arange(n_heads * LANES)[None, :] // LANES == hrow).astype(BF16)
    dskip_full = jnp.repeat(d_skip.astype(F32), HEAD_DIM).reshape(1, d_ssd)

    blk = lambda w: pl.BlockSpec((t, w), lambda b, c: (b * nc + c, 0))
    full = lambda arr: pl.BlockSpec(arr.shape, lambda b, c: (0,) * arr.ndim)
    consts = (conv_w.astype(F32), conv_b.reshape(1, d_conv).astype(F32), pad_row(dt_bias), pad_row(a_log),
              dskip_full, norm_w.reshape(1, d_ssd).astype(F32), shifts, ltri, e64, e128)
    return pl.pallas_call(
        functools.partial(_ssd_kernel, d_ssd),
        grid=(bsz, nc),
        in_specs=[blk(d_ssd), blk(d_conv), blk(LANES)] + [full(a) for a in consts],
        out_specs=blk(d_ssd),
        out_shape=jax.ShapeDtypeStruct((t_all, d_ssd), BF16),
        scratch_shapes=[pltpu.VMEM((2, t, d_conv), BF16), pltpu.VMEM((SSD_STATE, d_ssd), F32)],
        compiler_params=_params(("arbitrary", "arbitrary")),
        name="ssd",
    )(z, xbc, dt, *consts)


def _sbattn_kernel(seq, group, q_ref, k_ref, v_ref, qw_ref, kw_ref, bd_ref, uo_ref, o_ref,
                   q2_scr, kn_scr, c_scr, acc_scr):
    blk = SB_BLOCK
    rows = group * blk
    chain = 2 * blk
    n_groups = seq // rows
    scale = HEAD_DIM ** -0.5
    nt_dims = (((1,), (1,)), ((), ()))

    low_rows = lax.broadcasted_iota(jnp.int32, (rows, LANES), 1) < HEAD_DIM

    def prep(i, carry):
        mq, mk = carry
        sl = pl.ds(pl.multiple_of(i * rows, rows), rows)

        def normed(ref, w_ref):
            xf = ref[sl, :].astype(F32)
            ms = _dot_split_lhs(xf * xf, bd_ref[...], 2) * (1.0 / HEAD_DIM)
            return xf * lax.rsqrt(ms + EPS) * w_ref[...]

        def sq_norm_max(xb):
            xf = xb.astype(F32)
            ss = _dot_split_lhs(xf * xf, bd_ref[...], 2)
            return jnp.max(ss.reshape(rows // 8, 8, LANES), axis=0)

        qn = (normed(q_ref, qw_ref) * scale).astype(BF16)
        kn = normed(k_ref, kw_ref).astype(BF16)
        zero = jnp.zeros_like(qn)
        q_lo = jnp.where(low_rows, qn, zero)
        q_hi = jnp.where(low_rows, zero, qn)
        for b in range(group):
            q2_scr[i * group + b, 0:blk, :] = q_lo[b * blk:(b + 1) * blk]
            q2_scr[i * group + b, blk:chain, :] = q_hi[b * blk:(b + 1) * blk]
        kn_scr[sl, :] = kn
        return jnp.maximum(mq, sq_norm_max(qn)), jnp.maximum(mk, sq_norm_max(kn))

    zeros8 = jnp.zeros((8, LANES), F32)
    mq, mk = lax.fori_loop(0, n_groups, prep, (zeros8, zeros8))
    logit_bound = jnp.sqrt(jnp.max(mq, keepdims=True) * jnp.max(mk, keepdims=True)) * 1.01 + 0.01
    stop_level = -(EXP_ZERO_BELOW + logit_bound)

    ri = lax.broadcasted_iota(jnp.int32, (group * chain, blk), 0)
    si = lax.broadcasted_iota(jnp.int32, (group * chain, blk), 1)
    strict = si < (ri & (blk - 1))
    low_blk = lax.broadcasted_iota(jnp.int32, (blk, LANES), 1) < HEAD_DIM
    uo = uo_ref[...]

    def step(grp, s, diagonal):
        zs, k0s, pens = [], [], []
        for i in range(group):
            qb = grp * group + i
            kb = qb - s
            k0 = pl.multiple_of(jnp.maximum(kb, 0) * blk, blk)
            k0s.append(k0)
            pens.append(jnp.where(kb >= 0, 0.0, NEG_BIG).astype(F32))
            zs.append(lax.dot_general(q2_scr[qb], kn_scr[pl.ds(k0, blk), :], nt_dims, preferred_element_type=F32))
        z = jnp.concatenate(zs, axis=0)
        lr = -_softplus(z)
        if diagonal:
            lr = jnp.where(strict, lr, 0.0)
        rt = _dot_split_lhs(lr, uo, 2)
        if diagonal:
            c_old = jnp.zeros_like(z)
        else:
            c_old = jnp.concatenate([c_scr[i * chain:(i + 1) * chain, :] + pens[i] for i in range(group)], axis=0)
        w = jnp.exp(z + rt[:, :blk] + c_old)
        if diagonal:
            w = jnp.where(strict, w, 0.0)
        c_new = c_old + rt[:, blk:]
        c_scr[...] = c_new
        wb = w.astype(BF16)
        for i in range(group):
            pv = jnp.dot(wb[i * chain:(i + 1) * chain], v_ref[pl.ds(k0s[i], blk), :], preferred_element_type=F32)
            if diagonal:
                acc_scr[i * chain:(i + 1) * chain, :] = pv
            else:
                acc_scr[i * chain:(i + 1) * chain, :] += pv
        return jnp.max(c_new, keepdims=True)

    def q_group(grp, _):
        last_qb = grp * group + group - 1

        def live(m, s_next):
            return jnp.logical_and((m - stop_level)[0, 0] >= 0.0, s_next <= last_qb)

        m0 = step(grp, 0, True)

        def body(carry):
            s, _ = carry
            m = step(grp, s, False)
            return s + 1, live(m, s + 1)

        lax.while_loop(lambda carry: carry[1], body, (jnp.int32(1), live(m0, jnp.int32(1))))
        for i in range(group):
            a = acc_scr[i * chain:(i + 1) * chain, :]
            q0 = pl.multiple_of((grp * group + i) * blk, blk)
            o_ref[pl.ds(q0, blk), :] = jnp.where(low_blk, a[0:blk], a[blk:chain]).astype(o_ref.dtype)
        return 0

    lax.fori_loop(0, n_groups, q_group, 0)


def _sbattn(q, k, v, q_norm_w, k_norm_w, bsz, seq):
    t_all, d_sb = q.shape
    pairs = d_sb // LANES
    blk = SB_BLOCK
    group = 4
    qw = jnp.tile(q_norm_w.astype(F32), LANES // HEAD_DIM).reshape(1, LANES)
    kw = jnp.tile(k_norm_w.astype(F32), LANES // HEAD_DIM).reshape(1, LANES)
    li = jnp.arange(LANES)
    bd = (li[:, None] // HEAD_DIM == li[None, :] // HEAD_DIM).astype(BF16)
    si = jnp.arange(blk)
    uo = jnp.concatenate([(si[:, None] >= si[None, :]), jnp.ones((blk, blk), bool)], axis=1).astype(BF16)
    io = pl.BlockSpec((seq, LANES), lambda b, p: (b, p))
    full = lambda arr: pl.BlockSpec(arr.shape, lambda b, p: (0,) * arr.ndim)
    return pl.pallas_call(
        functools.partial(_sbattn_kernel, seq, group),
        grid=(bsz, pairs),
        in_specs=[io, io, io, full(qw), full(kw), full(bd), full(uo)],
        out_specs=io,
        out_shape=jax.ShapeDtypeStruct((t_all, d_sb), BF16),
        scratch_shapes=[pltpu.VMEM((seq // blk, 2 * blk, LANES), BF16), pltpu.VMEM((seq, LANES), BF16),
                        pltpu.VMEM((group * 2 * blk, blk), F32), pltpu.VMEM((group * 2 * blk, LANES), F32)],
        compiler_params=_params(("arbitrary", "arbitrary")),
        name="sbattn",
    )(q, k, v, qw, kw, bd, uo)


def _outffn_kernel(ff_chunk, x_ref, ys_ref, yb_ref, mod_ref, nw_ref, wo_ref, wg_ref, wu_ref, wd_ref, o_ref):
    d_ssd = ys_ref.shape[1]
    mix = jnp.dot(ys_ref[...], wo_ref[0:d_ssd, :], preferred_element_type=F32)
    mix = mix + jnp.dot(yb_ref[...], wo_ref[d_ssd:, :], preferred_element_type=F32)
    x1 = x_ref[...] + mod_ref[0, 2:3, :] * mix
    ms = jnp.mean(x1 * x1, axis=-1, keepdims=True)
    y = x1 * lax.rsqrt(ms + EPS) * nw_ref[...]
    h = (y * (1.0 + mod_ref[0, 4:5, :]) + mod_ref[0, 3:4, :]).astype(BF16)
    d_ff = wg_ref.shape[1]
    ffn = None
    for s in range(0, d_ff, ff_chunk):
        gate = jnp.dot(h, wg_ref[:, s:s + ff_chunk], preferred_element_type=F32)
        up = jnp.dot(h, wu_ref[:, s:s + ff_chunk], preferred_element_type=F32)
        act = (_silu(gate) * up).astype(BF16)
        d = jnp.dot(act, wd_ref[s:s + ff_chunk, :], preferred_element_type=F32)
        ffn = d if ffn is None else ffn + d
    o_ref[...] = x1 + mod_ref[0, 5:6, :] * ffn


def _outffn(x2d, y_ssd, y_sb, mod3, norm_w, w_out, w_gate, w_up, w_down, seq, tm):
    t, d = x2d.shape
    per_b = seq // tm
    d_ff = w_gate.shape[1]
    ff_chunk = d_ff // 2
    row = lambda w: pl.BlockSpec((tm, w), lambda i: (i, 0))
    resident = lambda arr: pl.BlockSpec(arr.shape, lambda i: (0, 0), pipeline_mode=pl.Buffered(1))
    return pl.pallas_call(
        functools.partial(_outffn_kernel, ff_chunk),
        grid=(t // tm,),
        in_specs=[row(d), row(y_ssd.shape[1]), row(y_sb.shape[1]),
                  pl.BlockSpec((1, N_MOD, d), lambda i: (i // per_b, 0, 0)),
                  pl.BlockSpec((1, d), lambda i: (0, 0)),
                  resident(w_out), resident(w_gate), resident(w_up), resident(w_down)],
        out_specs=row(d),
        out_shape=jax.ShapeDtypeStruct((t, d), F32),
        compiler_params=_params(("arbitrary",)),
        name="outffn",
    )(x2d, y_ssd, y_sb, mod3, norm_w.reshape(1, d), w_out, w_gate, w_up, w_down)


def kernel(x, c, w_ada, b_ada, norm1_w, w_in, conv_w, conv_b, dt_bias, a_log, d_skip, ssd_norm_w, q_norm_w,
           k_norm_w, w_out, norm2_w, w_gate, w_up, w_down):
    bsz, seq, d = x.shape
    depth = w_ada.shape[0]
    n_heads = dt_bias.shape[1]
    d_ssd = n_heads * HEAD_DIM
    d_conv = conv_w.shape[2]
    d_sb = (w_in.shape[2] - d_ssd - d_conv - n_heads) // 3
    dt_lo = d_ssd + d_conv
    splits = (d_ssd, d_conv, LANES, d_sb, d_sb, d_sb)
    tm = 512

    x2d = x.reshape(bsz * seq, d)
    cond_in = c.astype(F32)
    for layer in range(depth):
        mod3 = _adaln(cond_in, w_ada[layer], b_ada[layer]).reshape(bsz, N_MOD, d)
        w_l = w_in[layer]
        w_dt = jnp.pad(w_l[:, dt_lo:dt_lo + n_heads], ((0, 0), (0, LANES - n_heads)))
        w_cat = jnp.concatenate([w_l[:, :dt_lo], w_dt, w_l[:, dt_lo + n_heads:]], axis=1).astype(BF16)
        z, xbc, dt, q, k, v = _inproj(x2d, mod3, norm1_w[layer], w_cat, splits, seq, tm)
        y_ssd = _ssd(z, xbc, dt, conv_w[layer], conv_b[layer], dt_bias[layer], a_log[layer], d_skip[layer],
                     ssd_norm_w[layer], bsz, seq)
        y_sb = _sbattn(q, k, v, q_norm_w[layer], k_norm_w[layer], bsz, seq)
        x2d = _outffn(x2d, y_ssd, y_sb, mod3, norm2_w[layer], w_out[layer].astype(BF16),
                      w_gate[layer].astype(BF16), w_up[layer].astype(BF16), w_down[layer].astype(BF16), seq, tm)
    return x2d.reshape(bsz, seq, d)
```

```python
import functools

import jax
import jax.numpy as jnp
from jax import lax
from jax.experimental import pallas as pl
from jax.experimental.pallas import tpu as pltpu

F32 = jnp.float32
BF16 = jnp.bfloat16

HEAD_DIM = 64
SSD_GROUPS = 2
SSD_STATE = 128
CONV_WIDTH = 4
SSD_CHUNK = 128
SB_BLOCK = 128
N_MOD = 6
EPS = 1e-6
LANES = 128
VMEM_LIMIT_BYTES = 56 * 1024 * 1024
NEG_BIG = -1e30
EXP2_ZERO_BELOW = 151.0
LOG2E = 1.4426950408889634
HEAD_ROWS = 16
EAGER_STEPS = 3


def _split_bf16(x, parts):
    out = []
    r = x
    for i in range(parts):
        p = r.astype(BF16)
        out.append(p)
        if i + 1 < parts:
            r = r - p.astype(F32)
    return out


def _dot_split_lhs(x, m, parts):
    acc = None
    for p in _split_bf16(x, parts):
        d = jnp.dot(p, m, preferred_element_type=F32)
        acc = d if acc is None else acc + d
    return acc


def _dot_split_rhs(m, x, parts):
    acc = None
    for p in _split_bf16(x, parts):
        d = jnp.dot(m, p, preferred_element_type=F32)
        acc = d if acc is None else acc + d
    return acc


def _silu(x):
    return x / (1.0 + jnp.exp(-x))


def _softplus(x):
    return jnp.maximum(x, 0.0) + jnp.log(1.0 + jnp.exp(-jnp.abs(x)))


def _params(sem):
    return pltpu.CompilerParams(dimension_semantics=sem, vmem_limit_bytes=VMEM_LIMIT_BYTES)


def _adaln_kernel(c_ref, w_ref, b_ref, o_ref):
    cond = _silu(c_ref[...])
    o_ref[...] = _dot_split_lhs_f32(cond, w_ref[...]) + b_ref[...]


def _dot_split_lhs_f32(x, w):
    xs = _split_bf16(x, 3)
    ws = _split_bf16(w, 3)
    acc = None
    for i in range(3):
        for j in range(3 - i):
            d = jnp.dot(xs[i], ws[j], preferred_element_type=F32)
            acc = d if acc is None else acc + d
    return acc


def _adaln(c, w, b):
    bsz, d = c.shape
    n = w.shape[1]
    tn = d
    return pl.pallas_call(
        _adaln_kernel,
        grid=(n // tn,),
        in_specs=[pl.BlockSpec((bsz, d), lambda j: (0, 0)),
                  pl.BlockSpec((d, tn), lambda j: (0, j)),
                  pl.BlockSpec((1, tn), lambda j: (0, j))],
        out_specs=pl.BlockSpec((bsz, tn), lambda j: (0, j)),
        out_shape=jax.ShapeDtypeStruct((bsz, n), F32),
        compiler_params=_params(("arbitrary",)),
        name="adaln",
    )(c, w, b.reshape(1, n))


def _inproj_kernel(splits, x_ref, mod_ref, nw_ref, wa_ref, wdt_ref, wb_ref,
                   z_ref, xbc_ref, dt_ref, q_ref, k_ref, v_ref, h_scr):
    x = x_ref[...]
    ms = jnp.mean(x * x, axis=-1, keepdims=True)
    y = x * lax.rsqrt(ms + EPS) * nw_ref[...]
    h_scr[...] = (y * (1.0 + mod_ref[0, 1:2, :]) + mod_ref[0, 0:1, :]).astype(BF16)
    plan = ((z_ref, wa_ref), (xbc_ref, wa_ref), (dt_ref, wdt_ref), (q_ref, wb_ref), (k_ref, wb_ref), (v_ref, wb_ref))
    offsets = {id(wa_ref): 0, id(wdt_ref): 0, id(wb_ref): 0}
    for (o_ref, w_ref), width in zip(plan, splits):
        lo = offsets[id(w_ref)]
        step = 512 if width % 512 == 0 else width
        for s in range(0, width, step):
            r = jnp.dot(h_scr[...], w_ref[:, lo + s:lo + s + step], preferred_element_type=F32)
            o_ref[:, s:s + step] = r.astype(o_ref.dtype)
        offsets[id(w_ref)] = lo + width


def _inproj(x2d, mod3, norm_w, w_a, w_dt, w_b, splits, seq, tm):
    t, d = x2d.shape
    per_b = seq // tm
    dtypes = (BF16, BF16, F32, BF16, BF16, BF16)
    resident = lambda arr: pl.BlockSpec(arr.shape, lambda i: (0, 0), pipeline_mode=pl.Buffered(1))
    return pl.pallas_call(
        functools.partial(_inproj_kernel, splits),
        grid=(t // tm,),
        in_specs=[pl.BlockSpec((tm, d), lambda i: (i, 0)),
                  pl.BlockSpec((1, N_MOD, d), lambda i: (i // per_b, 0, 0)),
                  pl.BlockSpec((1, d), lambda i: (0, 0)),
                  resident(w_a), resident(w_dt), resident(w_b)],
        out_specs=[pl.BlockSpec((tm, w), lambda i: (i, 0)) for w in splits],
        out_shape=[jax.ShapeDtypeStruct((t, w), dt) for w, dt in zip(splits, dtypes)],
        scratch_shapes=[pltpu.VMEM((tm, d), BF16)],
        compiler_params=_params(("arbitrary",)),
        name="inproj",
    )(x2d, mod3, norm_w.reshape(1, d), w_a, w_dt, w_b)


def _ssd_kernel(d_ssd, z_ref, xbc_ref, dt_ref, cw_ref, cb_ref, dtb_ref, alog_ref, dskip_ref, nw_ref,
                shift_ref, ltri_ref, e64_ref, e128_ref, o_ref, xbc_scr, state_scr):
    t = SSD_CHUNK
    n = SSD_STATE
    gw = d_ssd // SSD_GROUPS
    chunk = pl.program_id(1)
    cur = chunk % 2

    @pl.when(chunk == 0)
    def _():
        xbc_scr[1] = jnp.zeros(xbc_scr.shape[1:], BF16)
        state_scr[...] = jnp.zeros(state_scr.shape, F32)

    xbc_scr[cur] = xbc_ref[...]
    u_cur = xbc_scr[cur]
    u_prev = xbc_scr[1 - cur]
    conv = cb_ref[...] + cw_ref[CONV_WIDTH - 1:CONV_WIDTH, :] * u_cur.astype(F32)
    head = None
    for k in range(CONV_WIDTH - 1):
        conv = conv + cw_ref[k:k + 1, :] * jnp.dot(shift_ref[k, :, t:2 * t], u_cur, preferred_element_type=F32)
        from_prev = cw_ref[k:k + 1, :] * jnp.dot(shift_ref[k, 0:HEAD_ROWS, 0:t], u_prev, preferred_element_type=F32)
        head = from_prev if head is None else head + from_prev
    conv = jnp.concatenate([conv[0:HEAD_ROWS] + head, conv[HEAD_ROWS:]], axis=0)
    act = _silu(conv)
    xs = act[:, :d_ssd]
    bm = act[:, d_ssd:d_ssd + SSD_GROUPS * n]
    cm = act[:, d_ssd + SSD_GROUPS * n:]

    dt = _softplus(dt_ref[...] + dtb_ref[...])
    a = -jnp.exp(alog_ref[...])
    a_cs = _dot_split_rhs(ltri_ref[...], a * dt, 3)
    e64 = e64_ref[...]
    dt_full = _dot_split_lhs(dt, e64, 2)
    acs_full = _dot_split_lhs(a_cs, e64, 2)
    acs_col = _dot_split_lhs(a_cs, e128_ref[...], 2)
    acs_row = a_cs.T
    a_last = acs_full[t - 1:t, :]
    xg = xs * dt_full
    xg_b = xg.astype(BF16)
    xdec_b = (xg * jnp.exp(a_last - acs_full)).astype(BF16)
    exp_acs = jnp.exp(acs_full)
    chunk_decay = jnp.exp(a_last)

    li = lax.broadcasted_iota(jnp.int32, (t, t), 0)
    si = lax.broadcasted_iota(jnp.int32, (t, t), 1)
    causal = li >= si
    lane = lax.broadcasted_iota(jnp.int32, (t, LANES), 1)
    low_half = lane < HEAD_DIM

    heads_per_group = gw // HEAD_DIM
    nt_dims = (((1,), (1,)), ((), ()))
    y_parts = []
    for g in range(SSD_GROUPS):
        bm_g = bm[:, g * n:(g + 1) * n]
        cm_g = cm[:, g * n:(g + 1) * n].astype(BF16)
        scores = lax.dot_general(cm_g, bm_g.astype(BF16), nt_dims, preferred_element_type=F32)
        st = state_scr[:, g * gw:(g + 1) * gw]
        y_off = jnp.dot(cm_g, st.astype(BF16), preferred_element_type=F32) * exp_acs[:, g * gw:(g + 1) * gw]
        new_states = jnp.dot(bm_g.T.astype(BF16), xdec_b[:, g * gw:(g + 1) * gw], preferred_element_type=F32)
        state_scr[:, g * gw:(g + 1) * gw] = st * chunk_decay[:, g * gw:(g + 1) * gw] + new_states
        for pr in range(heads_per_group // 2):
            col = g * gw + pr * LANES
            xg_pair = xg_b[:, col:col + LANES]
            y_pair = None
            for j in range(2):
                h = (col // HEAD_DIM) + j
                seg = acs_col[:, h * LANES:(h + 1) * LANES] - acs_row[h:h + 1, :]
                decay = jnp.exp(jnp.where(causal, seg, NEG_BIG))
                m_h = (scores * decay).astype(BF16)
                x_h = jnp.where(low_half if j == 0 else jnp.logical_not(low_half), xg_pair, jnp.zeros_like(xg_pair))
                d = jnp.dot(m_h, x_h, preferred_element_type=F32)
                y_pair = d if y_pair is None else y_pair + d
            y_parts.append(y_pair + y_off[:, pr * LANES:(pr + 1) * LANES])
    y = jnp.concatenate(y_parts, axis=-1) + dskip_ref[...] * xs

    y = y * _silu(z_ref[...].astype(F32))
    outs = []
    for g in range(SSD_GROUPS):
        yg = y[:, g * gw:(g + 1) * gw]
        ms = jnp.mean(yg * yg, axis=-1, keepdims=True)
        outs.append(yg * lax.rsqrt(ms + EPS))
    o_ref[...] = (jnp.concatenate(outs, axis=-1) * nw_ref[...]).astype(o_ref.dtype)


def _ssd(z, xbc, dt, conv_w, conv_b, dt_bias, a_log, d_skip, norm_w, bsz, seq):
    t_all, d_ssd = z.shape
    d_conv = xbc.shape[1]
    n_heads = d_ssd // HEAD_DIM
    t = SSD_CHUNK
    nc = seq // t

    def pad_row(v):
        return jnp.zeros((1, LANES), F32).at[0, :n_heads].set(v.astype(F32))

    rows = jnp.arange(t)[:, None]
    cols = jnp.arange(2 * t)[None, :]
    shifts = jnp.stack([(cols == rows + t - (CONV_WIDTH - 1) + k) for k in range(CONV_WIDTH - 1)]).astype(BF16)
    ltri = (jnp.arange(t)[None, :] <= rows).astype(BF16)
    hrow = jnp.arange(LANES)[:, None]
    e64 = (jnp.arange(d_ssd)[None, :] // HEAD_DIM == hrow).astype(BF16)
    e128 = (jnp.arange(n_heads * LANES)[None, :] // LANES == hrow).astype(BF16)
    dskip_full = jnp.repeat(d_skip.astype(F32), HEAD_DIM).reshape(1, d_ssd)

    blk = lambda w: pl.BlockSpec((t, w), lambda b, c: (b * nc + c, 0))
    full = lambda arr: pl.BlockSpec(arr.shape, lambda b, c: (0,) * arr.ndim)
    consts = (conv_w.astype(F32), conv_b.reshape(1, d_conv).astype(F32), pad_row(dt_bias), pad_row(a_log),
              dskip_full, norm_w.reshape(1, d_ssd).astype(F32), shifts, ltri, e64, e128)
    return pl.pallas_call(
        functools.partial(_ssd_kernel, d_ssd),
        grid=(bsz, nc),
        in_specs=[blk(d_ssd), blk(d_conv), blk(LANES)] + [full(a) for a in consts],
        out_specs=blk(d_ssd),
        out_shape=jax.ShapeDtypeStruct((t_all, d_ssd), BF16),
        scratch_shapes=[pltpu.VMEM((2, t, d_conv), BF16), pltpu.VMEM((SSD_STATE, d_ssd), F32)],
        compiler_params=_params(("arbitrary", "arbitrary")),
        name="ssd",
    )(z, xbc, dt, *consts)


def _sbattn_kernel(seq, group, q_ref, k_ref, v_ref, qw_ref, kw_ref, bd_ref, uo_ref, o_ref,
                   q2_scr, kn_scr, c_scr, acc_scr):
    blk = SB_BLOCK
    rows = group * blk
    chain = 2 * blk
    n_groups = seq // rows
    scale = HEAD_DIM ** -0.5
    nt_dims = (((1,), (1,)), ((), ()))

    low_rows = lax.broadcasted_iota(jnp.int32, (rows, LANES), 1) < HEAD_DIM

    def prep(i, _):
        sl = pl.ds(pl.multiple_of(i * rows, rows), rows)

        def normed(ref, w_ref):
            xf = ref[sl, :].astype(F32)
            ms = _dot_split_lhs(xf * xf, bd_ref[...], 2) * (1.0 / HEAD_DIM)
            return xf * lax.rsqrt(ms + EPS) * w_ref[...]

        qn = (normed(q_ref, qw_ref) * (scale * LOG2E)).astype(BF16)
        zero = jnp.zeros_like(qn)
        q_lo = jnp.where(low_rows, qn, zero)
        q_hi = jnp.where(low_rows, zero, qn)
        for b in range(group):
            q2_scr[i * group + b, 0:blk, :] = q_lo[b * blk:(b + 1) * blk]
            q2_scr[i * group + b, blk:chain, :] = q_hi[b * blk:(b + 1) * blk]
        kn_scr[sl, :] = normed(k_ref, kw_ref).astype(BF16)
        return 0

    lax.fori_loop(0, n_groups, prep, 0)
    w_bound = jnp.max(jnp.abs(qw_ref[...]), keepdims=True) * jnp.max(jnp.abs(kw_ref[...]), keepdims=True)
    logit_bound = w_bound * (HEAD_DIM * scale * LOG2E * 1.01) + 0.01
    stop_level = -(EXP2_ZERO_BELOW + logit_bound)

    ri = lax.broadcasted_iota(jnp.int32, (group * chain, blk), 0)
    si = lax.broadcasted_iota(jnp.int32, (group * chain, blk), 1)
    strict = si < (ri & (blk - 1))
    low_blk = lax.broadcasted_iota(jnp.int32, (blk, LANES), 1) < HEAD_DIM
    uo = uo_ref[...]

    def step(grp, s, diagonal):
        zs, k0s, pens = [], [], []
        for i in range(group):
            qb = grp * group + i
            kb = qb - s
            k0 = pl.multiple_of(jnp.maximum(kb, 0) * blk, blk)
            k0s.append(k0)
            pens.append(jnp.where(kb >= 0, 0.0, NEG_BIG).astype(F32))
            zs.append(lax.dot_general(q2_scr[qb], kn_scr[pl.ds(k0, blk), :], nt_dims, preferred_element_type=F32))
        z = jnp.concatenate(zs, axis=0)
        nz = -z
        lr = jnp.minimum(nz, 0.0) - jnp.log2(1.0 + jnp.exp2(jnp.minimum(z, nz)))
        if diagonal:
            lr = jnp.where(strict, lr, 0.0)
        rt = _dot_split_lhs(lr, uo, 2)
        if diagonal:
            c_old = jnp.zeros_like(z)
        else:
            c_old = jnp.concatenate([c_scr[i * chain:(i + 1) * chain, :] + pens[i] for i in range(group)], axis=0)
        w = jnp.exp2(z + rt[:, :blk] + c_old)
        if diagonal:
            w = jnp.where(strict, w, 0.0)
        c_new = c_old + rt[:, blk:]
        c_scr[...] = c_new
        wb = w.astype(BF16)
        for i in range(group):
            pv = jnp.dot(wb[i * chain:(i + 1) * chain], v_ref[pl.ds(k0s[i], blk), :], preferred_element_type=F32)
            if diagonal:
                acc_scr[i * chain:(i + 1) * chain, :] = pv
            else:
                acc_scr[i * chain:(i + 1) * chain, :] += pv
        return jnp.max(c_new, keepdims=True)

    def q_group(grp, _):
        last_qb = grp * group + group - 1

        def live(m, s_next):
            return jnp.logical_and((m - stop_level)[0, 0] >= 0.0, s_next <= last_qb)

        step(grp, 0, True)
        for s in range(1, EAGER_STEPS - 1):
            step(grp, s, False)
        m = step(grp, EAGER_STEPS - 1, False)

        def body(carry):
            s, _ = carry
            m = step(grp, s, False)
            return s + 1, live(m, s + 1)

        first = jnp.int32(EAGER_STEPS)
        lax.while_loop(lambda carry: carry[1], body, (first, live(m, first)))
        for i in range(group):
            a = acc_scr[i * chain:(i + 1) * chain, :]
            q0 = pl.multiple_of((grp * group + i) * blk, blk)
            o_ref[pl.ds(q0, blk), :] = jnp.where(low_blk, a[0:blk], a[blk:chain]).astype(o_ref.dtype)
        return 0

    lax.fori_loop(0, n_groups, q_group, 0)


def _sbattn(q, k, v, q_norm_w, k_norm_w, bsz, seq):
    t_all, d_sb = q.shape
    pairs = d_sb // LANES
    blk = SB_BLOCK
    group = 4
    qw = jnp.tile(q_norm_w.astype(F32), LANES // HEAD_DIM).reshape(1, LANES)
    kw = jnp.tile(k_norm_w.astype(F32), LANES // HEAD_DIM).reshape(1, LANES)
    li = jnp.arange(LANES)
    bd = (li[:, None] // HEAD_DIM == li[None, :] // HEAD_DIM).astype(BF16)
    si = jnp.arange(blk)
    uo = jnp.concatenate([(si[:, None] >= si[None, :]), jnp.ones((blk, blk), bool)], axis=1).astype(BF16)
    io = pl.BlockSpec((seq, LANES), lambda b, p: (b, p))
    full = lambda arr: pl.BlockSpec(arr.shape, lambda b, p: (0,) * arr.ndim)
    return pl.pallas_call(
        functools.partial(_sbattn_kernel, seq, group),
        grid=(bsz, pairs),
        in_specs=[io, io, io, full(qw), full(kw), full(bd), full(uo)],
        out_specs=io,
        out_shape=jax.ShapeDtypeStruct((t_all, d_sb), BF16),
        scratch_shapes=[pltpu.VMEM((seq // blk, 2 * blk, LANES), BF16), pltpu.VMEM((seq, LANES), BF16),
                        pltpu.VMEM((group * 2 * blk, blk), F32), pltpu.VMEM((group * 2 * blk, LANES), F32)],
        compiler_params=_params(("arbitrary", "arbitrary")),
        name="sbattn",
    )(q, k, v, qw, kw, bd, uo)


def _outffn_kernel(ff_chunk, x_ref, ys_ref, yb_ref, mod_ref, nw_ref, wo_ref, wg_ref, wu_ref, wd_ref, o_ref):
    d_ssd = ys_ref.shape[1]
    mix = jnp.dot(ys_ref[...], wo_ref[0:d_ssd, :], preferred_element_type=F32)
    mix = mix + jnp.dot(yb_ref[...], wo_ref[d_ssd:, :], preferred_element_type=F32)
    x1 = x_ref[...] + mod_ref[0, 2:3, :] * mix
    ms = jnp.mean(x1 * x1, axis=-1, keepdims=True)
    y = x1 * lax.rsqrt(ms + EPS) * nw_ref[...]
    h = (y * (1.0 + mod_ref[0, 4:5, :]) + mod_ref[0, 3:4, :]).astype(BF16)
    d_ff = wg_ref.shape[1]
    ffn = None
    for s in range(0, d_ff, ff_chunk):
        gate = jnp.dot(h, wg_ref[:, s:s + ff_chunk], preferred_element_type=F32)
        up = jnp.dot(h, wu_ref[:, s:s + ff_chunk], preferred_element_type=F32)
        act = (_silu(gate) * up).astype(BF16)
        d = jnp.dot(act, wd_ref[s:s + ff_chunk, :], preferred_element_type=F32)
        ffn = d if ffn is None else ffn + d
    o_ref[...] = x1 + mod_ref[0, 5:6, :] * ffn


def _outffn(x2d, y_ssd, y_sb, mod3, norm_w, w_out, w_gate, w_up, w_down, seq, tm):
    t, d = x2d.shape
    per_b = seq // tm
    d_ff = w_gate.shape[1]
    ff_chunk = d_ff // 2
    row = lambda w: pl.BlockSpec((tm, w), lambda i: (i, 0))
    resident = lambda arr: pl.BlockSpec(arr.shape, lambda i: (0, 0), pipeline_mode=pl.Buffered(1))
    return pl.pallas_call(
        functools.partial(_outffn_kernel, ff_chunk),
        grid=(t // tm,),
        in_specs=[row(d), row(y_ssd.shape[1]), row(y_sb.shape[1]),
                  pl.BlockSpec((1, N_MOD, d), lambda i: (i // per_b, 0, 0)),
                  pl.BlockSpec((1, d), lambda i: (0, 0)),
                  resident(w_out), resident(w_gate), resident(w_up), resident(w_down)],
        out_specs=row(d),
        out_shape=jax.ShapeDtypeStruct((t, d), F32),
        compiler_params=_params(("arbitrary",)),
        name="outffn",
    )(x2d, y_ssd, y_sb, mod3, norm_w.reshape(1, d), w_out, w_gate, w_up, w_down)


def kernel(x, c, w_ada, b_ada, norm1_w, w_in, conv_w, conv_b, dt_bias, a_log, d_skip, ssd_norm_w, q_norm_w,
           k_norm_w, w_out, norm2_w, w_gate, w_up, w_down):
    bsz, seq, d = x.shape
    depth = w_ada.shape[0]
    n_heads = dt_bias.shape[1]
    d_ssd = n_heads * HEAD_DIM
    d_conv = conv_w.shape[2]
    d_sb = (w_in.shape[2] - d_ssd - d_conv - n_heads) // 3
    dt_lo = d_ssd + d_conv
    splits = (d_ssd, d_conv, LANES, d_sb, d_sb, d_sb)
    tm = 512

    x2d = x.reshape(bsz * seq, d)
    cond_in = c.astype(F32)
    for layer in range(depth):
        mod3 = _adaln(cond_in, w_ada[layer], b_ada[layer]).reshape(bsz, N_MOD, d)
        w_l = w_in[layer]
        w_dt = jnp.pad(w_l[:, dt_lo:dt_lo + n_heads], ((0, 0), (0, LANES - n_heads))).astype(BF16)
        w_a = w_l[:, :dt_lo].astype(BF16)
        w_b = w_l[:, dt_lo + n_heads:].astype(BF16)
        z, xbc, dt, q, k, v = _inproj(x2d, mod3, norm1_w[layer], w_a, w_dt, w_b, splits, seq, tm)
        y_ssd = _ssd(z, xbc, dt, conv_w[layer], conv_b[layer], dt_bias[layer], a_log[layer], d_skip[layer],
                     ssd_norm_w[layer], bsz, seq)
        y_sb = _sbattn(q, k, v, q_norm_w[layer], k_norm_w[layer], bsz, seq)
        x2d = _outffn(x2d, y_ssd, y_sb, mod3, norm2_w[layer], w_out[layer].astype(BF16),
                      w_gate[layer].astype(BF16), w_up[layer].astype(BF16), w_down[layer].astype(BF16), seq, tm)
    return x2d.reshape(bsz, seq, d)
```

```python
import functools

import jax
import jax.numpy as jnp
from jax import lax
from jax.experimental import pallas as pl
from jax.experimental.pallas import tpu as pltpu

F32 = jnp.float32
BF16 = jnp.bfloat16

HEAD_DIM = 64
SSD_GROUPS = 2
SSD_STATE = 128
CONV_WIDTH = 4
SSD_CHUNK = 128
SB_BLOCK = 128
N_MOD = 6
EPS = 1e-6
LANES = 128
VMEM_LIMIT_BYTES = 56 * 1024 * 1024
NEG_BIG = -1e30
EXP2_ZERO_BELOW = 151.0
LOG2E = 1.4426950408889634
HEAD_ROWS = 16
EAGER_STEPS = 3


def _split_bf16(x, parts):
    out = []
    r = x
    for i in range(parts):
        p = r.astype(BF16)
        out.append(p)
        if i + 1 < parts:
            r = r - p.astype(F32)
    return out


def _dot_split_lhs(x, m, parts):
    acc = None
    for p in _split_bf16(x, parts):
        d = jnp.dot(p, m, preferred_element_type=F32)
        acc = d if acc is None else acc + d
    return acc


def _dot_split_rhs(m, x, parts):
    acc = None
    for p in _split_bf16(x, parts):
        d = jnp.dot(m, p, preferred_element_type=F32)
        acc = d if acc is None else acc + d
    return acc


def _silu(x):
    return x / (1.0 + jnp.exp(-x))


def _softplus(x):
    return jnp.maximum(x, 0.0) + jnp.log(1.0 + jnp.exp(-jnp.abs(x)))


def _params(sem):
    return pltpu.CompilerParams(dimension_semantics=sem, vmem_limit_bytes=VMEM_LIMIT_BYTES)


def _adaln_kernel(c_ref, w_ref, b_ref, o_ref):
    cond = _silu(c_ref[...])
    o_ref[...] = _dot_split_lhs_f32(cond, w_ref[...]) + b_ref[...]


def _dot_split_lhs_f32(x, w):
    xs = _split_bf16(x, 3)
    ws = _split_bf16(w, 3)
    acc = None
    for i in range(3):
        for j in range(3 - i):
            d = jnp.dot(xs[i], ws[j], preferred_element_type=F32)
            acc = d if acc is None else acc + d
    return acc


def _adaln(c, w, b):
    bsz, d = c.shape
    n = w.shape[1]
    tn = d
    return pl.pallas_call(
        _adaln_kernel,
        grid=(n // tn,),
        in_specs=[pl.BlockSpec((bsz, d), lambda j: (0, 0)),
                  pl.BlockSpec((d, tn), lambda j: (0, j)),
                  pl.BlockSpec((1, tn), lambda j: (0, j))],
        out_specs=pl.BlockSpec((bsz, tn), lambda j: (0, j)),
        out_shape=jax.ShapeDtypeStruct((bsz, n), F32),
        compiler_params=_params(("arbitrary",)),
        name="adaln",
    )(c, w, b.reshape(1, n))


def _inproj_kernel(splits, x_ref, mod_ref, nw_ref, wa_ref, wdt_ref, wb_ref,
                   z_ref, xbc_ref, dt_ref, q_ref, k_ref, v_ref, h_scr):
    x = x_ref[...]
    ms = jnp.mean(x * x, axis=-1, keepdims=True)
    y = x * lax.rsqrt(ms + EPS) * nw_ref[...]
    h_scr[...] = (y * (1.0 + mod_ref[0, 1:2, :]) + mod_ref[0, 0:1, :]).astype(BF16)
    plan = ((z_ref, wa_ref), (xbc_ref, wa_ref), (dt_ref, wdt_ref), (q_ref, wb_ref), (k_ref, wb_ref), (v_ref, wb_ref))
    offsets = {id(wa_ref): 0, id(wdt_ref): 0, id(wb_ref): 0}
    for (o_ref, w_ref), width in zip(plan, splits):
        lo = offsets[id(w_ref)]
        step = 512 if width % 512 == 0 else width
        for s in range(0, width, step):
            r = jnp.dot(h_scr[...], w_ref[:, lo + s:lo + s + step], preferred_element_type=F32)
            o_ref[:, s:s + step] = r.astype(o_ref.dtype)
        offsets[id(w_ref)] = lo + width


def _inproj(x2d, mod3, norm_w, w_a, w_dt, w_b, splits, seq, tm):
    t, d = x2d.shape
    per_b = seq // tm
    dtypes = (BF16, BF16, F32, BF16, BF16, BF16)
    resident = lambda arr: pl.BlockSpec(arr.shape, lambda i: (0, 0), pipeline_mode=pl.Buffered(1))
    return pl.pallas_call(
        functools.partial(_inproj_kernel, splits),
        grid=(t // tm,),
        in_specs=[pl.BlockSpec((tm, d), lambda i: (i, 0)),
                  pl.BlockSpec((1, N_MOD, d), lambda i: (i // per_b, 0, 0)),
                  pl.BlockSpec((1, d), lambda i: (0, 0)),
                  resident(w_a), resident(w_dt), resident(w_b)],
        out_specs=[pl.BlockSpec((tm, w), lambda i: (i, 0)) for w in splits],
        out_shape=[jax.ShapeDtypeStruct((t, w), dt) for w, dt in zip(splits, dtypes)],
        scratch_shapes=[pltpu.VMEM((tm, d), BF16)],
        compiler_params=_params(("arbitrary",)),
        name="inproj",
    )(x2d, mod3, norm_w.reshape(1, d), w_a, w_dt, w_b)


def _ssd_kernel(d_ssd, z_ref, xbc_ref, dt_ref, cw_ref, cb_ref, dtb_ref, alog_ref, dskip_ref, nw_ref,
                shift_ref, ltri_ref, e64_ref, e128_ref, o_ref, xbc_scr, state_scr):
    t = SSD_CHUNK
    n = SSD_STATE
    gw = d_ssd // SSD_GROUPS
    chunk = pl.program_id(1)
    cur = chunk % 2

    @pl.when(chunk == 0)
    def _():
        xbc_scr[1] = jnp.zeros(xbc_scr.shape[1:], BF16)
        state_scr[...] = jnp.zeros(state_scr.shape, F32)

    xbc_scr[cur] = xbc_ref[...]
    u_cur = xbc_scr[cur]
    u_prev = xbc_scr[1 - cur]
    conv = cb_ref[...] + cw_ref[CONV_WIDTH - 1:CONV_WIDTH, :] * u_cur.astype(F32)
    taps = CONV_WIDTH - 1
    sh_cur = jnp.dot(shift_ref[:, :, t:2 * t].reshape(taps * t, t), u_cur, preferred_element_type=F32)
    sh_prev = jnp.dot(shift_ref[:, 0:HEAD_ROWS, 0:t].reshape(taps * HEAD_ROWS, t), u_prev,
                      preferred_element_type=F32)
    head = None
    for k in range(taps):
        conv = conv + cw_ref[k:k + 1, :] * sh_cur[k * t:(k + 1) * t]
        from_prev = cw_ref[k:k + 1, :] * sh_prev[k * HEAD_ROWS:(k + 1) * HEAD_ROWS]
        head = from_prev if head is None else head + from_prev
    conv = jnp.concatenate([conv[0:HEAD_ROWS] + head, conv[HEAD_ROWS:]], axis=0)
    act = _silu(conv)
    xs = act[:, :d_ssd]
    bm = act[:, d_ssd:d_ssd + SSD_GROUPS * n]
    cm = act[:, d_ssd + SSD_GROUPS * n:]

    dt = _softplus(dt_ref[...] + dtb_ref[...])
    a = -jnp.exp(alog_ref[...])
    a_cs = _dot_split_rhs(ltri_ref[...], a * dt, 3)
    e64 = e64_ref[...]
    dt_full = _dot_split_lhs(dt, e64, 2)
    acs_full = _dot_split_lhs(a_cs, e64, 2)
    acs_col = _dot_split_lhs(a_cs, e128_ref[...], 2)
    acs_row = a_cs.T
    a_last = acs_full[t - 1:t, :]
    xg = xs * dt_full
    xg_b = xg.astype(BF16)
    xdec_b = (xg * jnp.exp(a_last - acs_full)).astype(BF16)
    exp_acs = jnp.exp(acs_full)
    chunk_decay = jnp.exp(a_last)

    li = lax.broadcasted_iota(jnp.int32, (t, t), 0)
    si = lax.broadcasted_iota(jnp.int32, (t, t), 1)
    causal = li >= si
    lane = lax.broadcasted_iota(jnp.int32, (t, LANES), 1)
    low_half = lane < HEAD_DIM

    heads_per_group = gw // HEAD_DIM
    nt_dims = (((1,), (1,)), ((), ()))
    y_parts = []
    for g in range(SSD_GROUPS):
        bm_g = bm[:, g * n:(g + 1) * n]
        cm_g = cm[:, g * n:(g + 1) * n].astype(BF16)
        scores = lax.dot_general(cm_g, bm_g.astype(BF16), nt_dims, preferred_element_type=F32)
        st = state_scr[:, g * gw:(g + 1) * gw]
        y_off = jnp.dot(cm_g, st.astype(BF16), preferred_element_type=F32) * exp_acs[:, g * gw:(g + 1) * gw]
        new_states = jnp.dot(bm_g.T.astype(BF16), xdec_b[:, g * gw:(g + 1) * gw], preferred_element_type=F32)
        state_scr[:, g * gw:(g + 1) * gw] = st * chunk_decay[:, g * gw:(g + 1) * gw] + new_states
        for pr in range(heads_per_group // 2):
            col = g * gw + pr * LANES
            xg_pair = xg_b[:, col:col + LANES]
            y_pair = None
            for j in range(2):
                h = (col // HEAD_DIM) + j
                seg = acs_col[:, h * LANES:(h + 1) * LANES] - acs_row[h:h + 1, :]
                decay = jnp.exp(jnp.where(causal, seg, NEG_BIG))
                m_h = (scores * decay).astype(BF16)
                x_h = jnp.where(low_half if j == 0 else jnp.logical_not(low_half), xg_pair, jnp.zeros_like(xg_pair))
                d = jnp.dot(m_h, x_h, preferred_element_type=F32)
                y_pair = d if y_pair is None else y_pair + d
            y_parts.append(y_pair + y_off[:, pr * LANES:(pr + 1) * LANES])
    y = jnp.concatenate(y_parts, axis=-1) + dskip_ref[...] * xs

    y = y * _silu(z_ref[...].astype(F32))
    outs = []
    for g in range(SSD_GROUPS):
        yg = y[:, g * gw:(g + 1) * gw]
        ms = jnp.mean(yg * yg, axis=-1, keepdims=True)
        outs.append(yg * lax.rsqrt(ms + EPS))
    o_ref[...] = (jnp.concatenate(outs, axis=-1) * nw_ref[...]).astype(o_ref.dtype)


def _ssd(z, xbc, dt, conv_w, conv_b, dt_bias, a_log, d_skip, norm_w, bsz, seq):
    t_all, d_ssd = z.shape
    d_conv = xbc.shape[1]
    n_heads = d_ssd // HEAD_DIM
    t = SSD_CHUNK
    nc = seq // t

    def pad_row(v):
        return jnp.zeros((1, LANES), F32).at[0, :n_heads].set(v.astype(F32))

    rows = jnp.arange(t)[:, None]
    cols = jnp.arange(2 * t)[None, :]
    shifts = jnp.stack([(cols == rows + t - (CONV_WIDTH - 1) + k) for k in range(CONV_WIDTH - 1)]).astype(BF16)
    ltri = (jnp.arange(t)[None, :] <= rows).astype(BF16)
    hrow = jnp.arange(LANES)[:, None]
    e64 = (jnp.arange(d_ssd)[None, :] // HEAD_DIM == hrow).astype(BF16)
    e128 = (jnp.arange(n_heads * LANES)[None, :] // LANES == hrow).astype(BF16)
    dskip_full = jnp.repeat(d_skip.astype(F32), HEAD_DIM).reshape(1, d_ssd)

    blk = lambda w: pl.BlockSpec((t, w), lambda b, c: (b * nc + c, 0))
    full = lambda arr: pl.BlockSpec(arr.shape, lambda b, c: (0,) * arr.ndim)
    consts = (conv_w.astype(F32), conv_b.reshape(1, d_conv).astype(F32), pad_row(dt_bias), pad_row(a_log),
              dskip_full, norm_w.reshape(1, d_ssd).astype(F32), shifts, ltri, e64, e128)
    return pl.pallas_call(
        functools.partial(_ssd_kernel, d_ssd),
        grid=(bsz, nc),
        in_specs=[blk(d_ssd), blk(d_conv), blk(LANES)] + [full(a) for a in consts],
        out_specs=blk(d_ssd),
        out_shape=jax.ShapeDtypeStruct((t_all, d_ssd), BF16),
        scratch_shapes=[pltpu.VMEM((2, t, d_conv), BF16), pltpu.VMEM((SSD_STATE, d_ssd), F32)],
        compiler_params=_params(("arbitrary", "arbitrary")),
        name="ssd",
    )(z, xbc, dt, *consts)


def _sbattn_kernel(seq, group, q_ref, k_ref, v_ref, qw_ref, kw_ref, bd_ref, uo_ref, o_ref,
                   q2_scr, kn_scr, c_scr, acc_scr):
    blk = SB_BLOCK
    rows = group * blk
    chain = 2 * blk
    n_groups = seq // rows
    scale = HEAD_DIM ** -0.5
    nt_dims = (((1,), (1,)), ((), ()))

    prep_blocks = 2 * group
    prep_rows = prep_blocks * blk
    low_rows = lax.broadcasted_iota(jnp.int32, (prep_rows, LANES), 1) < HEAD_DIM

    def prep(i, _):
        sl = pl.ds(pl.multiple_of(i * prep_rows, prep_rows), prep_rows)

        def normed(ref, w_ref):
            xf = ref[sl, :].astype(F32)
            ms = _dot_split_lhs(xf * xf, bd_ref[...], 2) * (1.0 / HEAD_DIM)
            return xf * lax.rsqrt(ms + EPS) * w_ref[...]

        qn = (normed(q_ref, qw_ref) * (scale * LOG2E)).astype(BF16)
        zero = jnp.zeros_like(qn)
        q_lo = jnp.where(low_rows, qn, zero)
        q_hi = jnp.where(low_rows, zero, qn)
        for b in range(prep_blocks):
            q2_scr[i * prep_blocks + b, 0:blk, :] = q_lo[b * blk:(b + 1) * blk]
            q2_scr[i * prep_blocks + b, blk:chain, :] = q_hi[b * blk:(b + 1) * blk]
        kn_scr[sl, :] = normed(k_ref, kw_ref).astype(BF16)
        return 0

    lax.fori_loop(0, seq // prep_rows, prep, 0)
    w_bound = jnp.max(jnp.abs(qw_ref[...]), keepdims=True) * jnp.max(jnp.abs(kw_ref[...]), keepdims=True)
    logit_bound = w_bound * (HEAD_DIM * scale * LOG2E * 1.01) + 0.01
    stop_level = -(EXP2_ZERO_BELOW + logit_bound)

    ri = lax.broadcasted_iota(jnp.int32, (group * chain, blk), 0)
    si = lax.broadcasted_iota(jnp.int32, (group * chain, blk), 1)
    strict = si < (ri & (blk - 1))
    low_blk = lax.broadcasted_iota(jnp.int32, (blk, LANES), 1) < HEAD_DIM
    uo = uo_ref[...]

    def step(grp, s, diagonal):
        zs, k0s, pens = [], [], []
        for i in range(group):
            qb = grp * group + i
            kb = qb - s
            k0 = pl.multiple_of(jnp.maximum(kb, 0) * blk, blk)
            k0s.append(k0)
            pens.append(jnp.where(kb >= 0, 0.0, NEG_BIG).astype(F32))
            zs.append(lax.dot_general(q2_scr[qb], kn_scr[pl.ds(k0, blk), :], nt_dims, preferred_element_type=F32))
        z = jnp.concatenate(zs, axis=0)
        nz = -z
        lr = jnp.minimum(nz, 0.0) - jnp.log2(1.0 + jnp.exp2(jnp.minimum(z, nz)))
        if diagonal:
            lr = jnp.where(strict, lr, 0.0)
        rt = _dot_split_lhs(lr, uo, 2)
        if diagonal:
            c_old = jnp.zeros_like(z)
        else:
            c_old = jnp.concatenate([c_scr[i * chain:(i + 1) * chain, :] + pens[i] for i in range(group)], axis=0)
        w = jnp.exp2(z + rt[:, :blk] + c_old)
        if diagonal:
            w = jnp.where(strict, w, 0.0)
        c_new = c_old + rt[:, blk:]
        c_scr[...] = c_new
        wb = w.astype(BF16)
        for i in range(group):
            pv = jnp.dot(wb[i * chain:(i + 1) * chain], v_ref[pl.ds(k0s[i], blk), :], preferred_element_type=F32)
            if diagonal:
                acc_scr[i * chain:(i + 1) * chain, :] = pv
            else:
                acc_scr[i * chain:(i + 1) * chain, :] += pv
        return jnp.max(c_new, keepdims=True)

    def q_group(grp, _):
        last_qb = grp * group + group - 1

        def live(m, s_next):
            return jnp.logical_and((m - stop_level)[0, 0] >= 0.0, s_next <= last_qb)

        step(grp, 0, True)
        for s in range(1, EAGER_STEPS - 1):
            step(grp, s, False)
        m = step(grp, EAGER_STEPS - 1, False)

        def body(carry):
            s, _ = carry
            m = step(grp, s, False)
            return s + 1, live(m, s + 1)

        first = jnp.int32(EAGER_STEPS)
        lax.while_loop(lambda carry: carry[1], body, (first, live(m, first)))
        for i in range(group):
            a = acc_scr[i * chain:(i + 1) * chain, :]
            q0 = pl.multiple_of((grp * group + i) * blk, blk)
            o_ref[pl.ds(q0, blk), :] = jnp.where(low_blk, a[0:blk], a[blk:chain]).astype(o_ref.dtype)
        return 0

    lax.fori_loop(0, n_groups, q_group, 0)


def _sbattn(q, k, v, q_norm_w, k_norm_w, bsz, seq):
    t_all, d_sb = q.shape
    pairs = d_sb // LANES
    blk = SB_BLOCK
    group = 4
    qw = jnp.tile(q_norm_w.astype(F32), LANES // HEAD_DIM).reshape(1, LANES)
    kw = jnp.tile(k_norm_w.astype(F32), LANES // HEAD_DIM).reshape(1, LANES)
    li = jnp.arange(LANES)
    bd = (li[:, None] // HEAD_DIM == li[None, :] // HEAD_DIM).astype(BF16)
    si = jnp.arange(blk)
    uo = jnp.concatenate([(si[:, None] >= si[None, :]), jnp.ones((blk, blk), bool)], axis=1).astype(BF16)
    io = pl.BlockSpec((seq, LANES), lambda b, p: (b, p))
    full = lambda arr: pl.BlockSpec(arr.shape, lambda b, p: (0,) * arr.ndim)
    return pl.pallas_call(
        functools.partial(_sbattn_kernel, seq, group),
        grid=(bsz, pairs),
        in_specs=[io, io, io, full(qw), full(kw), full(bd), full(uo)],
        out_specs=io,
        out_shape=jax.ShapeDtypeStruct((t_all, d_sb), BF16),
        scratch_shapes=[pltpu.VMEM((seq // blk, 2 * blk, LANES), BF16), pltpu.VMEM((seq, LANES), BF16),
                        pltpu.VMEM((group * 2 * blk, blk), F32), pltpu.VMEM((group * 2 * blk, LANES), F32)],
        compiler_params=_params(("arbitrary", "arbitrary")),
        name="sbattn",
    )(q, k, v, qw, kw, bd, uo)


def _outffn_kernel(ff_chunk, x_ref, ys_ref, yb_ref, mod_ref, nw_ref, wo_ref, wg_ref, wu_ref, wd_ref, o_ref):
    d_ssd = ys_ref.shape[1]
    mix = jnp.dot(ys_ref[...], wo_ref[0:d_ssd, :], preferred_element_type=F32)
    mix = mix + jnp.dot(yb_ref[...], wo_ref[d_ssd:, :], preferred_element_type=F32)
    x1 = x_ref[...] + mod_ref[0, 2:3, :] * mix
    ms = jnp.mean(x1 * x1, axis=-1, keepdims=True)
    y = x1 * lax.rsqrt(ms + EPS) * nw_ref[...]
    h = (y * (1.0 + mod_ref[0, 4:5, :]) + mod_ref[0, 3:4, :]).astype(BF16)
    d_ff = wg_ref.shape[1]
    ffn = None
    for s in range(0, d_ff, ff_chunk):
        gate = jnp.dot(h, wg_ref[:, s:s + ff_chunk], preferred_element_type=F32)
        up = jnp.dot(h, wu_ref[:, s:s + ff_chunk], preferred_element_type=F32)
        act = (_silu(gate) * up).astype(BF16)
        d = jnp.dot(act, wd_ref[s:s + ff_chunk, :], preferred_element_type=F32)
        ffn = d if ffn is None else ffn + d
    o_ref[...] = x1 + mod_ref[0, 5:6, :] * ffn


def _outffn(x2d, y_ssd, y_sb, mod3, norm_w, w_out, w_gate, w_up, w_down, seq, tm):
    t, d = x2d.shape
    per_b = seq // tm
    d_ff = w_gate.shape[1]
    ff_chunk = d_ff // 2
    row = lambda w: pl.BlockSpec((tm, w), lambda i: (i, 0))
    resident = lambda arr: pl.BlockSpec(arr.shape, lambda i: (0, 0), pipeline_mode=pl.Buffered(1))
    return pl.pallas_call(
        functools.partial(_outffn_kernel, ff_chunk),
        grid=(t // tm,),
        in_specs=[row(d), row(y_ssd.shape[1]), row(y_sb.shape[1]),
                  pl.BlockSpec((1, N_MOD, d), lambda i: (i // per_b, 0, 0)),
                  pl.BlockSpec((1, d), lambda i: (0, 0)),
                  resident(w_out), resident(w_gate), resident(w_up), resident(w_down)],
        out_specs=row(d),
        out_shape=jax.ShapeDtypeStruct((t, d), F32),
        compiler_params=_params(("arbitrary",)),
        name="outffn",
    )(x2d, y_ssd, y_sb, mod3, norm_w.reshape(1, d), w_out, w_gate, w_up, w_down)


def kernel(x, c, w_ada, b_ada, norm1_w, w_in, conv_w, conv_b, dt_bias, a_log, d_skip, ssd_norm_w, q_norm_w,
           k_norm_w, w_out, norm2_w, w_gate, w_up, w_down):
    bsz, seq, d = x.shape
    depth = w_ada.shape[0]
    n_heads = dt_bias.shape[1]
    d_ssd = n_heads * HEAD_DIM
    d_conv = conv_w.shape[2]
    d_sb = (w_in.shape[2] - d_ssd - d_conv - n_heads) // 3
    dt_lo = d_ssd + d_conv
    splits = (d_ssd, d_conv, LANES, d_sb, d_sb, d_sb)
    tm = 512

    x2d = x.reshape(bsz * seq, d)
    cond_in = c.astype(F32)
    for layer in range(depth):
        mod3 = _adaln(cond_in, w_ada[layer], b_ada[layer]).reshape(bsz, N_MOD, d)
        w_l = w_in[layer]
        w_dt = jnp.pad(w_l[:, dt_lo:dt_lo + n_heads], ((0, 0), (0, LANES - n_heads))).astype(BF16)
        w_a = w_l[:, :dt_lo].astype(BF16)
        w_b = w_l[:, dt_lo + n_heads:].astype(BF16)
        z, xbc, dt, q, k, v = _inproj(x2d, mod3, norm1_w[layer], w_a, w_dt, w_b, splits, seq, tm)
        y_ssd = _ssd(z, xbc, dt, conv_w[layer], conv_b[layer], dt_bias[layer], a_log[layer], d_skip[layer],
                     ssd_norm_w[layer], bsz, seq)
        y_sb = _sbattn(q, k, v, q_norm_w[layer], k_norm_w[layer], bsz, seq)
        x2d = _outffn(x2d, y_ssd, y_sb, mod3, norm2_w[layer], w_out[layer].astype(BF16),
                      w_gate[layer].astype(BF16), w_up[layer].astype(BF16), w_down[layer].astype(BF16), seq, tm)
    return x2d.reshape(bsz, seq, d)
```

```python
import functools

import jax
import jax.numpy as jnp
from jax import lax
from jax.experimental import pallas as pl
from jax.experimental.pallas import tpu as pltpu

F32 = jnp.float32
BF16 = jnp.bfloat16

HEAD_DIM = 64
SSD_GROUPS = 2
SSD_STATE = 128
CONV_WIDTH = 4
SSD_CHUNK = 128
SB_BLOCK = 128
N_MOD = 6
EPS = 1e-6
LANES = 128
VMEM_LIMIT_BYTES = 56 * 1024 * 1024
NEG_BIG = -1e30
EXP2_ZERO_BELOW = 151.0
LOG2E = 1.4426950408889634
HEAD_ROWS = 16
EAGER_STEPS = 3


def _split_bf16(x, parts):
    out = []
    r = x
    for i in range(parts):
        p = r.astype(BF16)
        out.append(p)
        if i + 1 < parts:
            r = r - p.astype(F32)
    return out


def _dot_split_lhs(x, m, parts):
    acc = None
    for p in _split_bf16(x, parts):
        d = jnp.dot(p, m, preferred_element_type=F32)
        acc = d if acc is None else acc + d
    return acc


def _dot_split_rhs(m, x, parts):
    acc = None
    for p in _split_bf16(x, parts):
        d = jnp.dot(m, p, preferred_element_type=F32)
        acc = d if acc is None else acc + d
    return acc


def _silu(x):
    return x / (1.0 + jnp.exp(-x))


def _softplus(x):
    return jnp.maximum(x, 0.0) + jnp.log(1.0 + jnp.exp(-jnp.abs(x)))


def _params(sem):
    return pltpu.CompilerParams(dimension_semantics=sem, vmem_limit_bytes=VMEM_LIMIT_BYTES)


def _adaln_kernel(c_ref, w_ref, b_ref, o_ref):
    cond = _silu(c_ref[...])
    o_ref[...] = _dot_split_lhs_f32(cond, w_ref[...]) + b_ref[...]


def _dot_split_lhs_f32(x, w):
    xs = _split_bf16(x, 3)
    ws = _split_bf16(w, 3)
    acc = None
    for i in range(3):
        for j in range(3 - i):
            d = jnp.dot(xs[i], ws[j], preferred_element_type=F32)
            acc = d if acc is None else acc + d
    return acc


def _adaln(c, w, b):
    bsz, d = c.shape
    n = w.shape[1]
    tn = d
    return pl.pallas_call(
        _adaln_kernel,
        grid=(n // tn,),
        in_specs=[pl.BlockSpec((bsz, d), lambda j: (0, 0)),
                  pl.BlockSpec((d, tn), lambda j: (0, j)),
                  pl.BlockSpec((1, tn), lambda j: (0, j))],
        out_specs=pl.BlockSpec((bsz, tn), lambda j: (0, j)),
        out_shape=jax.ShapeDtypeStruct((bsz, n), F32),
        compiler_params=_params(("arbitrary",)),
        name="adaln",
    )(c, w, b.reshape(1, n))


def _inproj_kernel(splits, x_ref, mod_ref, nw_ref, wa_ref, wdt_ref, wb_ref,
                   z_ref, xbc_ref, dt_ref, q_ref, k_ref, v_ref, h_scr):
    x = x_ref[...]
    ms = jnp.mean(x * x, axis=-1, keepdims=True)
    y = x * lax.rsqrt(ms + EPS) * nw_ref[...]
    h_scr[...] = (y * (1.0 + mod_ref[0, 1:2, :]) + mod_ref[0, 0:1, :]).astype(BF16)
    plan = ((z_ref, wa_ref), (xbc_ref, wa_ref), (dt_ref, wdt_ref), (q_ref, wb_ref), (k_ref, wb_ref), (v_ref, wb_ref))
    offsets = {id(wa_ref): 0, id(wdt_ref): 0, id(wb_ref): 0}
    for (o_ref, w_ref), width in zip(plan, splits):
        lo = offsets[id(w_ref)]
        step = 512 if width % 512 == 0 else width
        for s in range(0, width, step):
            r = jnp.dot(h_scr[...], w_ref[:, lo + s:lo + s + step], preferred_element_type=F32)
            o_ref[:, s:s + step] = r.astype(o_ref.dtype)
        offsets[id(w_ref)] = lo + width


def _inproj(x2d, mod3, norm_w, w_a, w_dt, w_b, splits, seq, tm):
    t, d = x2d.shape
    per_b = seq // tm
    dtypes = (BF16, BF16, F32, BF16, BF16, BF16)
    resident = lambda arr: pl.BlockSpec(arr.shape, lambda i: (0, 0), pipeline_mode=pl.Buffered(1))
    return pl.pallas_call(
        functools.partial(_inproj_kernel, splits),
        grid=(t // tm,),
        in_specs=[pl.BlockSpec((tm, d), lambda i: (i, 0)),
                  pl.BlockSpec((1, N_MOD, d), lambda i: (i // per_b, 0, 0)),
                  pl.BlockSpec((1, d), lambda i: (0, 0)),
                  resident(w_a), resident(w_dt), resident(w_b)],
        out_specs=[pl.BlockSpec((tm, w), lambda i: (i, 0)) for w in splits],
        out_shape=[jax.ShapeDtypeStruct((t, w), dt) for w, dt in zip(splits, dtypes)],
        scratch_shapes=[pltpu.VMEM((tm, d), BF16)],
        compiler_params=_params(("arbitrary",)),
        name="inproj",
    )(x2d, mod3, norm_w.reshape(1, d), w_a, w_dt, w_b)


def _ssd_kernel(d_ssd, z_ref, xbc_ref, dt_ref, cw_ref, cb_ref, dtb_ref, alog_ref, dskip_ref, nw_ref,
                shift_ref, ltri_ref, e64_ref, e128_ref, o_ref, xbc_scr, state_scr):
    t = SSD_CHUNK
    n = SSD_STATE
    gw = d_ssd // SSD_GROUPS
    chunk = pl.program_id(1)
    cur = chunk % 2

    @pl.when(chunk == 0)
    def _():
        xbc_scr[1] = jnp.zeros(xbc_scr.shape[1:], BF16)
        state_scr[...] = jnp.zeros(state_scr.shape, F32)

    xbc_scr[cur] = xbc_ref[...]
    u_cur = xbc_scr[cur]
    u_prev = xbc_scr[1 - cur]
    conv = cb_ref[...] + cw_ref[CONV_WIDTH - 1:CONV_WIDTH, :] * u_cur.astype(F32)
    taps = CONV_WIDTH - 1
    sh_cur = jnp.dot(shift_ref[:, :, t:2 * t].reshape(taps * t, t), u_cur, preferred_element_type=F32)
    sh_prev = jnp.dot(shift_ref[:, 0:HEAD_ROWS, 0:t].reshape(taps * HEAD_ROWS, t), u_prev,
                      preferred_element_type=F32)
    head = None
    for k in range(taps):
        conv = conv + cw_ref[k:k + 1, :] * sh_cur[k * t:(k + 1) * t]
        from_prev = cw_ref[k:k + 1, :] * sh_prev[k * HEAD_ROWS:(k + 1) * HEAD_ROWS]
        head = from_prev if head is None else head + from_prev
    conv = jnp.concatenate([conv[0:HEAD_ROWS] + head, conv[HEAD_ROWS:]], axis=0)
    act = _silu(conv)
    xs = act[:, :d_ssd]
    bm = act[:, d_ssd:d_ssd + SSD_GROUPS * n]
    cm = act[:, d_ssd + SSD_GROUPS * n:]

    dt = _softplus(dt_ref[...] + dtb_ref[...])
    a = -jnp.exp(alog_ref[...])
    a_cs = _dot_split_rhs(ltri_ref[...], a * dt, 3)
    e64 = e64_ref[...]
    dt_full = _dot_split_lhs(dt, e64, 2)
    acs_full = _dot_split_lhs(a_cs, e64, 2)
    acs_col = _dot_split_lhs(a_cs, e128_ref[...], 2)
    acs_row = a_cs.T
    a_last = acs_full[t - 1:t, :]
    xg = xs * dt_full
    xg_b = xg.astype(BF16)
    xdec_b = (xg * jnp.exp(a_last - acs_full)).astype(BF16)
    exp_acs = jnp.exp(acs_full)
    chunk_decay = jnp.exp(a_last)

    li = lax.broadcasted_iota(jnp.int32, (t, t), 0)
    si = lax.broadcasted_iota(jnp.int32, (t, t), 1)
    causal = li >= si
    lane = lax.broadcasted_iota(jnp.int32, (t, LANES), 1)
    low_half = lane < HEAD_DIM

    heads_per_group = gw // HEAD_DIM
    nt_dims = (((1,), (1,)), ((), ()))
    y_parts = []
    for g in range(SSD_GROUPS):
        bm_g = bm[:, g * n:(g + 1) * n]
        cm_g = cm[:, g * n:(g + 1) * n].astype(BF16)
        scores = lax.dot_general(cm_g, bm_g.astype(BF16), nt_dims, preferred_element_type=F32)
        st = state_scr[:, g * gw:(g + 1) * gw]
        y_off = jnp.dot(cm_g, st.astype(BF16), preferred_element_type=F32) * exp_acs[:, g * gw:(g + 1) * gw]
        new_states = jnp.dot(bm_g.T.astype(BF16), xdec_b[:, g * gw:(g + 1) * gw], preferred_element_type=F32)
        state_scr[:, g * gw:(g + 1) * gw] = st * chunk_decay[:, g * gw:(g + 1) * gw] + new_states
        for pr in range(heads_per_group // 2):
            col = g * gw + pr * LANES
            xg_pair = xg_b[:, col:col + LANES]
            y_pair = None
            for j in range(2):
                h = (col // HEAD_DIM) + j
                seg = acs_col[:, h * LANES:(h + 1) * LANES] - acs_row[h:h + 1, :]
                decay = jnp.exp(jnp.where(causal, seg, NEG_BIG))
                m_h = (scores * decay).astype(BF16)
                x_h = jnp.where(low_half if j == 0 else jnp.logical_not(low_half), xg_pair, jnp.zeros_like(xg_pair))
                d = jnp.dot(m_h, x_h, preferred_element_type=F32)
                y_pair = d if y_pair is None else y_pair + d
            y_parts.append(y_pair + y_off[:, pr * LANES:(pr + 1) * LANES])
    y = jnp.concatenate(y_parts, axis=-1) + dskip_ref[...] * xs

    y = y * _silu(z_ref[...].astype(F32))
    outs = []
    for g in range(SSD_GROUPS):
        yg = y[:, g * gw:(g + 1) * gw]
        ms = jnp.mean(yg * yg, axis=-1, keepdims=True)
        outs.append(yg * lax.rsqrt(ms + EPS))
    o_ref[...] = (jnp.concatenate(outs, axis=-1) * nw_ref[...]).astype(o_ref.dtype)


def _ssd(z, xbc, dt, conv_w, conv_b, dt_bias, a_log, d_skip, norm_w, bsz, seq):
    t_all, d_ssd = z.shape
    d_conv = xbc.shape[1]
    n_heads = d_ssd // HEAD_DIM
    t = SSD_CHUNK
    nc = seq // t

    def pad_row(v):
        return jnp.zeros((1, LANES), F32).at[0, :n_heads].set(v.astype(F32))

    rows = jnp.arange(t)[:, None]
    cols = jnp.arange(2 * t)[None, :]
    shifts = jnp.stack([(cols == rows + t - (CONV_WIDTH - 1) + k) for k in range(CONV_WIDTH - 1)]).astype(BF16)
    ltri = (jnp.arange(t)[None, :] <= rows).astype(BF16)
    hrow = jnp.arange(LANES)[:, None]
    e64 = (jnp.arange(d_ssd)[None, :] // HEAD_DIM == hrow).astype(BF16)
    e128 = (jnp.arange(n_heads * LANES)[None, :] // LANES == hrow).astype(BF16)
    dskip_full = jnp.repeat(d_skip.astype(F32), HEAD_DIM).reshape(1, d_ssd)

    blk = lambda w: pl.BlockSpec((t, w), lambda b, c: (b * nc + c, 0))
    full = lambda arr: pl.BlockSpec(arr.shape, lambda b, c: (0,) * arr.ndim)
    consts = (conv_w.astype(F32), conv_b.reshape(1, d_conv).astype(F32), pad_row(dt_bias), pad_row(a_log),
              dskip_full, norm_w.reshape(1, d_ssd).astype(F32), shifts, ltri, e64, e128)
    return pl.pallas_call(
        functools.partial(_ssd_kernel, d_ssd),
        grid=(bsz, nc),
        in_specs=[blk(d_ssd), blk(d_conv), blk(LANES)] + [full(a) for a in consts],
        out_specs=blk(d_ssd),
        out_shape=jax.ShapeDtypeStruct((t_all, d_ssd), BF16),
        scratch_shapes=[pltpu.VMEM((2, t, d_conv), BF16), pltpu.VMEM((SSD_STATE, d_ssd), F32)],
        compiler_params=_params(("arbitrary", "arbitrary")),
        name="ssd",
    )(z, xbc, dt, *consts)


def _sbattn_kernel(seq, group, q_ref, k_ref, v_ref, qw_ref, kw_ref, bd_ref, uo_ref, o_ref,
                   q2_scr, kn_scr, c_scr, acc_scr):
    blk = SB_BLOCK
    rows = group * blk
    chain = 2 * blk
    n_groups = seq // rows
    scale = HEAD_DIM ** -0.5
    nt_dims = (((1,), (1,)), ((), ()))

    prep_blocks = 2 * group
    prep_rows = prep_blocks * blk
    low_rows = lax.broadcasted_iota(jnp.int32, (prep_rows, LANES), 1) < HEAD_DIM

    def prep(i, _):
        sl = pl.ds(pl.multiple_of(i * prep_rows, prep_rows), prep_rows)

        def normed(ref, w_ref):
            xf = ref[sl, :].astype(F32)
            ms = _dot_split_lhs(xf * xf, bd_ref[...], 2) * (1.0 / HEAD_DIM)
            return xf * lax.rsqrt(ms + EPS) * w_ref[...]

        qn = (normed(q_ref, qw_ref) * (scale * LOG2E)).astype(BF16)
        zero = jnp.zeros_like(qn)
        q_lo = jnp.where(low_rows, qn, zero)
        q_hi = jnp.where(low_rows, zero, qn)
        for b in range(prep_blocks):
            q2_scr[i * prep_blocks + b, 0:blk, :] = q_lo[b * blk:(b + 1) * blk]
            q2_scr[i * prep_blocks + b, blk:chain, :] = q_hi[b * blk:(b + 1) * blk]
        kn_scr[sl, :] = normed(k_ref, kw_ref).astype(BF16)
        return 0

    lax.fori_loop(0, seq // prep_rows, prep, 0)
    w_bound = jnp.max(jnp.abs(qw_ref[...]), keepdims=True) * jnp.max(jnp.abs(kw_ref[...]), keepdims=True)
    logit_bound = w_bound * (HEAD_DIM * scale * LOG2E * 1.01) + 0.01
    stop_level = -(EXP2_ZERO_BELOW + logit_bound)

    ri = lax.broadcasted_iota(jnp.int32, (group * chain, blk), 0)
    si = lax.broadcasted_iota(jnp.int32, (group * chain, blk), 1)
    strict = si < (ri & (blk - 1))
    low_blk = lax.broadcasted_iota(jnp.int32, (blk, LANES), 1) < HEAD_DIM
    uo = uo_ref[...]

    def step(grp, s, diagonal):
        zs, k0s, pens = [], [], []
        for i in range(group):
            qb = grp * group + i
            kb = qb - s
            k0 = pl.multiple_of(jnp.maximum(kb, 0) * blk, blk)
            k0s.append(k0)
            pens.append(jnp.where(kb >= 0, 0.0, NEG_BIG).astype(F32))
            zs.append(lax.dot_general(q2_scr[qb], kn_scr[pl.ds(k0, blk), :], nt_dims, preferred_element_type=F32))
        z = jnp.concatenate(zs, axis=0)
        if diagonal:
            z = jnp.where(strict, z, NEG_BIG)
        nz = -z
        lr = jnp.minimum(nz, 0.0) - jnp.log2(1.0 + jnp.exp2(jnp.minimum(z, nz)))
        rt = _dot_split_lhs(lr, uo, 2)
        if diagonal:
            c_old = jnp.zeros_like(z)
        else:
            c_old = jnp.concatenate([c_scr[i * chain:(i + 1) * chain, :] + pens[i] for i in range(group)], axis=0)
        w = jnp.exp2(z + rt[:, :blk] + c_old)
        c_new = c_old + rt[:, blk:]
        c_scr[...] = c_new
        wb = w.astype(BF16)
        for i in range(group):
            pv = jnp.dot(wb[i * chain:(i + 1) * chain], v_ref[pl.ds(k0s[i], blk), :], preferred_element_type=F32)
            if diagonal:
                acc_scr[i * chain:(i + 1) * chain, :] = pv
            else:
                acc_scr[i * chain:(i + 1) * chain, :] += pv
        return jnp.max(c_new, keepdims=True)

    def q_group(grp, _):
        last_qb = grp * group + group - 1

        def live(m, s_next):
            return jnp.logical_and((m - stop_level)[0, 0] >= 0.0, s_next <= last_qb)

        step(grp, 0, True)
        for s in range(1, EAGER_STEPS - 1):
            step(grp, s, False)
        m = step(grp, EAGER_STEPS - 1, False)

        def body(carry):
            s, _ = carry
            m = step(grp, s, False)
            return s + 1, live(m, s + 1)

        first = jnp.int32(EAGER_STEPS)
        lax.while_loop(lambda carry: carry[1], body, (first, live(m, first)))
        for i in range(group):
            a = acc_scr[i * chain:(i + 1) * chain, :]
            q0 = pl.multiple_of((grp * group + i) * blk, blk)
            o_ref[pl.ds(q0, blk), :] = jnp.where(low_blk, a[0:blk], a[blk:chain]).astype(o_ref.dtype)
        return 0

    lax.fori_loop(0, n_groups, q_group, 0)


def _sbattn(q, k, v, q_norm_w, k_norm_w, bsz, seq):
    t_all, d_sb = q.shape
    pairs = d_sb // LANES
    blk = SB_BLOCK
    group = 4
    qw = jnp.tile(q_norm_w.astype(F32), LANES // HEAD_DIM).reshape(1, LANES)
    kw = jnp.tile(k_norm_w.astype(F32), LANES // HEAD_DIM).reshape(1, LANES)
    li = jnp.arange(LANES)
    bd = (li[:, None] // HEAD_DIM == li[None, :] // HEAD_DIM).astype(BF16)
    si = jnp.arange(blk)
    uo = jnp.concatenate([(si[:, None] >= si[None, :]), jnp.ones((blk, blk), bool)], axis=1).astype(BF16)
    io = pl.BlockSpec((seq, LANES), lambda b, p: (b, p))
    full = lambda arr: pl.BlockSpec(arr.shape, lambda b, p: (0,) * arr.ndim)
    return pl.pallas_call(
        functools.partial(_sbattn_kernel, seq, group),
        grid=(bsz, pairs),
        in_specs=[io, io, io, full(qw), full(kw), full(bd), full(uo)],
        out_specs=io,
        out_shape=jax.ShapeDtypeStruct((t_all, d_sb), BF16),
        scratch_shapes=[pltpu.VMEM((seq // blk, 2 * blk, LANES), BF16), pltpu.VMEM((seq, LANES), BF16),
                        pltpu.VMEM((group * 2 * blk, blk), F32), pltpu.VMEM((group * 2 * blk, LANES), F32)],
        compiler_params=_params(("arbitrary", "arbitrary")),
        name="sbattn",
    )(q, k, v, qw, kw, bd, uo)


def _outffn_kernel(ff_chunk, x_ref, ys_ref, yb_ref, mod_ref, nw_ref, wo_ref, wg_ref, wu_ref, wd_ref, o_ref):
    d_ssd = ys_ref.shape[1]
    mix = jnp.dot(ys_ref[...], wo_ref[0:d_ssd, :], preferred_element_type=F32)
    mix = mix + jnp.dot(yb_ref[...], wo_ref[d_ssd:, :], preferred_element_type=F32)
    x1 = x_ref[...] + mod_ref[0, 2:3, :] * mix
    ms = jnp.mean(x1 * x1, axis=-1, keepdims=True)
    y = x1 * lax.rsqrt(ms + EPS) * nw_ref[...]
    h = (y * (1.0 + mod_ref[0, 4:5, :]) + mod_ref[0, 3:4, :]).astype(BF16)
    d_ff = wg_ref.shape[1]
    ffn = None
    for s in range(0, d_ff, ff_chunk):
        gate = jnp.dot(h, wg_ref[:, s:s + ff_chunk], preferred_element_type=F32)
        up = jnp.dot(h, wu_ref[:, s:s + ff_chunk], preferred_element_type=F32)
        act = (_silu(gate) * up).astype(BF16)
        d = jnp.dot(act, wd_ref[s:s + ff_chunk, :], preferred_element_type=F32)
        ffn = d if ffn is None else ffn + d
    o_ref[...] = x1 + mod_ref[0, 5:6, :] * ffn


def _outffn(x2d, y_ssd, y_sb, mod3, norm_w, w_out, w_gate, w_up, w_down, seq, tm):
    t, d = x2d.shape
    per_b = seq // tm
    d_ff = w_gate.shape[1]
    ff_chunk = d_ff // 2
    row = lambda w: pl.BlockSpec((tm, w), lambda i: (i, 0))
    resident = lambda arr: pl.BlockSpec(arr.shape, lambda i: (0, 0), pipeline_mode=pl.Buffered(1))
    return pl.pallas_call(
        functools.partial(_outffn_kernel, ff_chunk),
        grid=(t // tm,),
        in_specs=[row(d), row(y_ssd.shape[1]), row(y_sb.shape[1]),
                  pl.BlockSpec((1, N_MOD, d), lambda i: (i // per_b, 0, 0)),
                  pl.BlockSpec((1, d), lambda i: (0, 0)),
                  resident(w_out), resident(w_gate), resident(w_up), resident(w_down)],
        out_specs=row(d),
        out_shape=jax.ShapeDtypeStruct((t, d), F32),
        compiler_params=_params(("arbitrary",)),
        name="outffn",
    )(x2d, y_ssd, y_sb, mod3, norm_w.reshape(1, d), w_out, w_gate, w_up, w_down)


def kernel(x, c, w_ada, b_ada, norm1_w, w_in, conv_w, conv_b, dt_bias, a_log, d_skip, ssd_norm_w, q_norm_w,
           k_norm_w, w_out, norm2_w, w_gate, w_up, w_down):
    bsz, seq, d = x.shape
    depth = w_ada.shape[0]
    n_heads = dt_bias.shape[1]
    d_ssd = n_heads * HEAD_DIM
    d_conv = conv_w.shape[2]
    d_sb = (w_in.shape[2] - d_ssd - d_conv - n_heads) // 3
    dt_lo = d_ssd + d_conv
    splits = (d_ssd, d_conv, LANES, d_sb, d_sb, d_sb)
    tm = 512

    x2d = x.reshape(bsz * seq, d)
    cond_in = c.astype(F32)
    for layer in range(depth):
        mod3 = _adaln(cond_in, w_ada[layer], b_ada[layer]).reshape(bsz, N_MOD, d)
        w_l = w_in[layer]
        w_dt = jnp.pad(w_l[:, dt_lo:dt_lo + n_heads], ((0, 0), (0, LANES - n_heads))).astype(BF16)
        w_a = w_l[:, :dt_lo].astype(BF16)
        w_b = w_l[:, dt_lo + n_heads:].astype(BF16)
        z, xbc, dt, q, k, v = _inproj(x2d, mod3, norm1_w[layer], w_a, w_dt, w_b, splits, seq, tm)
        y_ssd = _ssd(z, xbc, dt, conv_w[layer], conv_b[layer], dt_bias[layer], a_log[layer], d_skip[layer],
                     ssd_norm_w[layer], bsz, seq)
        y_sb = _sbattn(q, k, v, q_norm_w[layer], k_norm_w[layer], bsz, seq)
        x2d = _outffn(x2d, y_ssd, y_sb, mod3, norm2_w[layer], w_out[layer].astype(BF16),
                      w_gate[layer].astype(BF16), w_up[layer].astype(BF16), w_down[layer].astype(BF16), seq, tm)
    return x2d.reshape(bsz, seq, d)
```
